```python
import jax, jax.numpy as jnp
from jax import lax
import numpy as np

D_MODEL = 2048
BATCH = 4
SEQ = 8192
DEPTH = 1
DEC_BATCH = 1
DEC_SEQ = 16384
PAST_LEN = 128

GRID_W = 64
NA_KH = 8
NA_KW = 16
H_A = 16
DH_A = 64
D_A = H_A * DH_A
H_B = 16
DH_B = 64
D_B = H_B * DH_B
CHUNK = 128
D_MIX = D_A + D_B
D_IN_MIX = 3 * D_A + 2 * D_B
D_FF = 5632
EPS = 1e-6

kernel_name = "hymba_natten_gmlp_macaron_encoder"


def rmsnorm(x, g):
    x32 = x.astype(jnp.float32)
    y = x32 * lax.rsqrt(jnp.mean(x32 * x32, axis=-1, keepdims=True) + EPS)
    return (y * g.astype(jnp.float32)).astype(x.dtype)


def swiglu(h, w_in, w_out):
    gate, up = jnp.split(h @ w_in, 2, axis=-1)
    return (jax.nn.silu(gate) * up) @ w_out


def neighbourhood_attention(q, k, v, rpb):
    b, t, h, dh = q.shape
    rows = t // GRID_W
    kh = min(NA_KH, rows)
    qg = q.reshape(b, rows, GRID_W, h, dh)
    kg = k.reshape(b, rows, GRID_W, h, dh)
    vg = v.reshape(b, rows, GRID_W, h, dh)
    r = jnp.arange(rows)
    row_start = jnp.clip(r - kh // 2, 0, rows - kh)
    key_rows = row_start[:, None] + jnp.arange(kh)[None, :]
    k_blk = jnp.take(kg, key_rows, axis=1)
    v_blk = jnp.take(vg, key_rows, axis=1)
    c = jnp.arange(GRID_W)
    col_start = jnp.clip(c - NA_KW // 2, 0, GRID_W - NA_KW)
    col_off = c[None, :] - c[:, None]
    col_valid = (c[None, :] >= col_start[:, None]) & (c[None, :] < col_start[:, None] + NA_KW)
    row_off = key_rows - r[:, None]
    ri = (row_off + NA_KH - 1)[:, None, :, None]
    ci = (jnp.clip(col_off, -(NA_KW - 1), NA_KW - 1) + NA_KW - 1)[None, :, None, :]
    bias = rpb[:, ri, ci].astype(jnp.float32)
    bias = jnp.where(col_valid[None, :, None, :], bias, -jnp.inf)
    scale = dh ** -0.5
    s = jnp.einsum('brqhd,brkwhd->bhrqkw', qg, k_blk).astype(jnp.float32) * scale + bias
    p = jax.nn.softmax(s.reshape(b, h, rows, GRID_W, kh * GRID_W), axis=-1)
    p = p.reshape(b, h, rows, GRID_W, kh, GRID_W).astype(v.dtype)
    o = jnp.einsum('bhrqkw,brkwhd->brqhd', p, v_blk)
    return o.reshape(b, t, h * dh)


def spatial_gating(u, g, gate_norm, w_s, b_s):
    b, t, _ = u.shape
    g = rmsnorm(g, gate_norm)
    gc = g.reshape(b, t // CHUNK, CHUNK, H_B, DH_B)
    mixed = jnp.einsum('hpq,bcqhd->bcphd', w_s, gc) + b_s.T[None, None, :, :, None]
    return u * mixed.reshape(b, t, D_B)


def encoder_layer(x, ffn1_norm, ffn1_w_in, ffn1_w_out, mix_norm, w_in_mix, q_norm, k_norm,
                  attn_rpb, gate_norm, w_spatial, b_spatial, out_norm_a, out_norm_b, w_out_mix,
                  ffn2_norm, ffn2_w_in, ffn2_w_out, final_norm):
    b, t, _ = x.shape
    x = x + 0.5 * swiglu(rmsnorm(x, ffn1_norm), ffn1_w_in, ffn1_w_out)
    h = rmsnorm(x, mix_norm)
    z = h @ w_in_mix
    q, k, v, u, g = jnp.split(z, [D_A, 2 * D_A, 3 * D_A, 3 * D_A + D_B], axis=-1)
    q = rmsnorm(q.reshape(b, t, H_A, DH_A), q_norm)
    k = rmsnorm(k.reshape(b, t, H_A, DH_A), k_norm)
    v = v.reshape(b, t, H_A, DH_A)
    a = neighbourhood_attention(q, k, v, attn_rpb)
    sg = spatial_gating(jax.nn.gelu(u, approximate=False), jax.nn.gelu(g, approximate=False),
                        gate_norm, w_spatial, b_spatial)
    o = jnp.concatenate([rmsnorm(a, out_norm_a), rmsnorm(sg, out_norm_b)], axis=-1) @ w_out_mix
    x = x + o
    x = x + 0.5 * swiglu(rmsnorm(x, ffn2_norm), ffn2_w_in, ffn2_w_out)
    return rmsnorm(x, final_norm)


def setup_inputs(seed: int = 0) -> dict:
    key = jax.random.key(seed)
    ks = jax.random.split(key, 22)
    f32 = jnp.float32

    def nrm(k, shape, scale):
        return jax.random.normal(k, shape, f32) * scale

    def gain(k, shape):
        return 1.0 + 0.05 * jax.random.normal(k, shape, f32)

    return {
        "x_prompt": nrm(ks[0], (BATCH, SEQ, D_MODEL), 1.0),
        "x_sample": nrm(ks[1], (DEC_BATCH, DEC_SEQ, D_MODEL), 1.0),
        "ffn1_norm": gain(ks[2], (DEPTH, D_MODEL)),
        "ffn1_w_in": nrm(ks[3], (DEPTH, D_MODEL, 2 * D_FF), D_MODEL ** -0.5),
        "ffn1_w_out": nrm(ks[4], (DEPTH, D_FF, D_MODEL), D_FF ** -0.5),
        "mix_norm": gain(ks[5], (DEPTH, D_MODEL)),
        "w_in_mix": nrm(ks[6], (DEPTH, D_MODEL, D_IN_MIX), D_MODEL ** -0.5),
        "q_norm": gain(ks[7], (DEPTH, DH_A)),
        "k_norm": gain(ks[8], (DEPTH, DH_A)),
        "attn_rpb": nrm(ks[9], (DEPTH, H_A, 2 * NA_KH - 1, 2 * NA_KW - 1), 0.1),
        "gate_norm": gain(ks[10], (DEPTH, D_B)),
        "w_spatial": nrm(ks[11], (DEPTH, H_B, CHUNK, CHUNK), CHUNK ** -0.5),
        "b_spatial": gain(ks[12], (DEPTH, H_B, CHUNK)),
        "out_norm_a": gain(ks[13], (DEPTH, D_A)),
        "out_norm_b": gain(ks[14], (DEPTH, D_B)),
        "w_out_mix": nrm(ks[15], (DEPTH, D_MIX, D_MODEL), D_MIX ** -0.5),
        "ffn2_norm": gain(ks[16], (DEPTH, D_MODEL)),
        "ffn2_w_in": nrm(ks[17], (DEPTH, D_MODEL, 2 * D_FF), D_MODEL ** -0.5),
        "ffn2_w_out": nrm(ks[18], (DEPTH, D_FF, D_MODEL), D_FF ** -0.5),
        "final_norm": gain(ks[19], (DEPTH, D_MODEL)),
    }


def reference(x_prompt, x_sample, ffn1_norm, ffn1_w_in, ffn1_w_out, mix_norm, w_in_mix, q_norm,
              k_norm, attn_rpb, gate_norm, w_spatial, b_spatial, out_norm_a, out_norm_b, w_out_mix,
              ffn2_norm, ffn2_w_in, ffn2_w_out, final_norm):
    def run(x):
        for l in range(DEPTH):
            x = encoder_layer(x, ffn1_norm[l], ffn1_w_in[l], ffn1_w_out[l], mix_norm[l], w_in_mix[l],
                              q_norm[l], k_norm[l], attn_rpb[l], gate_norm[l], w_spatial[l],
                              b_spatial[l], out_norm_a[l], out_norm_b[l], w_out_mix[l],
                              ffn2_norm[l], ffn2_w_in[l], ffn2_w_out[l], final_norm[l])
        return x

    y_prompt = run(x_prompt)
    y_sample = run(x_sample)
    return (y_prompt, y_sample)
```

```python
import functools

import jax
import jax.numpy as jnp
from jax import lax
from jax.experimental import pallas as pl
from jax.experimental.pallas import tpu as pltpu

EPS = 1e-6
GRID_W = 64
NA_KH = 8
NA_KW = 16
H_A = 16
DH_A = 64
H_B = 16
DH_B = 64
CHUNK = 128
LANES = 128
PAIR_W = 2 * DH_A
N_PAIR = H_A // 2
MXU_DIM = 256
V7X_VMEM_BYTES = 64 * 1024 * 1024
VMEM_LIMIT = V7X_VMEM_BYTES - 6 * 1024 * 1024

F32 = jnp.float32
BF16 = jnp.bfloat16


def _rms(x, g):
    ms = jnp.mean(x * x, axis=-1, keepdims=True)
    return x * lax.rsqrt(ms + EPS) * g


def _gelu(x):
    return 0.5 * x * (1.0 + lax.erf(x * (2.0 ** -0.5)))


def _dot(a, b):
    return jnp.dot(a, b, preferred_element_type=F32)


def _pick(total, pref):
    b = min(total, pref)
    while total % b:
        b //= 2
    return b


def _ffn_kernel(x_ref, nrm_ref, wg_ref, wu_ref, wo_ref, *rest, final_norm):
    if final_norm:
        fn_ref, o_ref, h_ref = rest
    else:
        o_ref, h_ref = rest
    k = pl.program_id(1)

    @pl.when(k == 0)
    def _():
        x = x_ref[...]
        h_ref[...] = _rms(x, nrm_ref[...]).astype(BF16)
        o_ref[...] = x

    h = h_ref[...]
    g = _dot(h, wg_ref[...])
    u = _dot(h, wu_ref[...])
    a = ((0.5 * g) * jax.nn.sigmoid(g) * u).astype(BF16)
    o_ref[...] += _dot(a, wo_ref[...])

    if final_norm:
        @pl.when(k == pl.num_programs(1) - 1)
        def _():
            o_ref[...] = _rms(o_ref[...], fn_ref[...])


def _ffn(x, nrm, w_in, w_out, final_nrm=None):
    n, d = x.shape
    d_ff = w_out.shape[0]
    tm = _pick(n, 1024)
    tf = _pick(d_ff, 512)
    nk = d_ff // tf
    final = final_nrm is not None
    in_specs = [
        pl.BlockSpec((tm, d), lambda i, k: (i, 0)),
        pl.BlockSpec((1, d), lambda i, k: (0, 0)),
        pl.BlockSpec((d, tf), lambda i, k: (0, k)),
        pl.BlockSpec((d, tf), lambda i, k: (0, k + nk)),
        pl.BlockSpec((tf, d), lambda i, k: (k, 0)),
    ]
    args = [x, nrm, w_in, w_in, w_out]
    if final:
        in_specs.append(pl.BlockSpec((1, d), lambda i, k: (0, 0)))
        args.append(final_nrm)
    return pl.pallas_call(
        functools.partial(_ffn_kernel, final_norm=final),
        grid=(n // tm, nk),
        in_specs=in_specs,
        out_specs=pl.BlockSpec((tm, d), lambda i, k: (i, 0)),
        out_shape=jax.ShapeDtypeStruct((n, d), F32),
        scratch_shapes=[pltpu.VMEM((tm, d), BF16)],
        compiler_params=pltpu.CompilerParams(
            dimension_semantics=("arbitrary", "arbitrary"),
            vmem_limit_bytes=VMEM_LIMIT),
        name="ffn_final" if final else "ffn",
    )(*args)


def _head_mean_sq(z, bd):
    z2 = z * z
    hi = z2.astype(BF16)
    lo = (z2 - hi.astype(F32)).astype(BF16)
    parts = []
    for c in range(z.shape[1] // MXU_DIM):
        sl = slice(c * MXU_DIM, (c + 1) * MXU_DIM)
        parts.append(_dot(hi[:, sl], bd) + _dot(lo[:, sl], bd))
    return jnp.concatenate(parts, axis=1)


def _mix_in_kernel(x_ref, nrm_ref, w_ref, qn_ref, kn_ref, gn_ref, bd_ref,
                   qkvg_ref, u_ref, h_ref):
    j = pl.program_id(1)

    @pl.when(j == 0)
    def _():
        h_ref[...] = _rms(x_ref[...], nrm_ref[...]).astype(BF16)

    z = _dot(h_ref[...], w_ref[...])

    def head_norm(gain):
        return z * lax.rsqrt(_head_mean_sq(z, bd_ref[...]) + EPS) * gain

    @pl.when(j == 0)
    def _():
        qkvg_ref[...] = (head_norm(qn_ref[...]) * (DH_A ** -0.5)).astype(BF16)

    @pl.when(j == 1)
    def _():
        qkvg_ref[...] = head_norm(kn_ref[...]).astype(BF16)

    @pl.when(j == 2)
    def _():
        qkvg_ref[...] = z.astype(BF16)

    @pl.when(j == 3)
    def _():
        u_ref[...] = _gelu(z)

    @pl.when(j == 4)
    def _():
        qkvg_ref[...] = _rms(_gelu(z), gn_ref[...]).astype(BF16)


def _mix_in(x, nrm, w, qn, kn, gn, bd):
    n, d = x.shape
    da = H_A * DH_A
    tm = _pick(n, 512)

    def slot(i, j):
        return (jnp.where(j < 3, j, jnp.where(j == 3, 2, 3)), i, 0)

    return pl.pallas_call(
        _mix_in_kernel,
        grid=(n // tm, 5),
        in_specs=[
            pl.BlockSpec((tm, d), lambda i, j: (i, 0)),
            pl.BlockSpec((1, d), lambda i, j: (0, 0)),
            pl.BlockSpec((d, da), lambda i, j: (0, j)),
            pl.BlockSpec((1, da), lambda i, j: (0, 0)),
            pl.BlockSpec((1, da), lambda i, j: (0, 0)),
            pl.BlockSpec((1, da), lambda i, j: (0, 0)),
            pl.BlockSpec((MXU_DIM, MXU_DIM), lambda i, j: (0, 0)),
        ],
        out_specs=[
            pl.BlockSpec((None, tm, da), slot),
            pl.BlockSpec((tm, da), lambda i, j: (i, 0)),
        ],
        out_shape=[
            jax.ShapeDtypeStruct((4, n, da), BF16),
            jax.ShapeDtypeStruct((n, da), F32),
        ],
        scratch_shapes=[pltpu.VMEM((tm, d), BF16)],
        compiler_params=pltpu.CompilerParams(
            dimension_semantics=("arbitrary", "arbitrary"),
            vmem_limit_bytes=VMEM_LIMIT),
        name="mix_in",
    )(x, nrm, w, qn, kn, gn, bd)


def _natten_kernel(q_ref, k_ref, v_ref, bias_ref, o_ref, *, rows, unroll):
    kh = min(NA_KH, rows)
    nkeys = kh * GRID_W
    lane = lax.broadcasted_iota(jnp.int32, (1, PAIR_W), 1)
    head0 = lane < DH_A

    def one_row(r):
        rs = jnp.clip(r - kh // 2, 0, rows - kh)
        q = q_ref[pl.ds(pl.multiple_of(r * GRID_W, GRID_W), GRID_W), :]
        kw = k_ref[pl.ds(pl.multiple_of(rs * GRID_W, GRID_W), nkeys), :]
        vw = v_ref[pl.ds(pl.multiple_of(rs * GRID_W, GRID_W), nkeys), :]
        zero = jnp.zeros_like(q)
        qm = jnp.concatenate([jnp.where(head0, q, zero), jnp.where(head0, zero, q)], axis=0)
        s = lax.dot_general(qm, kw, (((1,), (1,)), ((), ())), preferred_element_type=F32)
        s = s + bias_ref[r - rs]
        m = jnp.max(s, axis=-1, keepdims=True)
        p = jnp.exp(s - m)
        l = jnp.sum(p, axis=-1, keepdims=True)
        o = _dot(p.astype(BF16), vw) / l
        o_ref[pl.ds(pl.multiple_of(r * GRID_W, GRID_W), GRID_W), :] = jnp.where(
            head0, o[:GRID_W], o[GRID_W:])

    def body(i, carry):
        for t in range(unroll):
            one_row(i * unroll + t)
        return carry

    lax.fori_loop(0, rows // unroll, body, 0)


def _natten(qkvg, bias, batch, seq):
    rows = seq // GRID_W
    kh = min(NA_KH, rows)
    da = H_A * DH_A
    unroll = 4 if rows % 4 == 0 else 1

    def qkv_spec(slot):
        return pl.BlockSpec((None, seq, PAIR_W), lambda b, p: (slot, b, p))

    return pl.pallas_call(
        functools.partial(_natten_kernel, rows=rows, unroll=unroll),
        grid=(batch, N_PAIR),
        in_specs=[
            qkv_spec(0), qkv_spec(1), qkv_spec(2),
            pl.BlockSpec((None, kh, PAIR_W, kh * GRID_W), lambda b, p: (p, 0, 0, 0)),
        ],
        out_specs=pl.BlockSpec((seq, PAIR_W), lambda b, p: (b, p)),
        out_shape=jax.ShapeDtypeStruct((batch * seq, da), F32),
        compiler_params=pltpu.CompilerParams(
            dimension_semantics=("arbitrary", "arbitrary"),
            vmem_limit_bytes=VMEM_LIMIT),
        name="natten",
    )(qkvg, qkvg, qkvg, bias)


def _attn_bias(rpb, rows):
    kh = min(NA_KH, rows)
    c = jnp.arange(GRID_W)
    col_start = jnp.clip(c - NA_KW // 2, 0, GRID_W - NA_KW)
    col_off = c[None, :] - c[:, None]
    col_valid = (c[None, :] >= col_start[:, None]) & (c[None, :] < col_start[:, None] + NA_KW)
    ci = jnp.clip(col_off, -(NA_KW - 1), NA_KW - 1) + NA_KW - 1
    var = jnp.arange(kh)
    ri = jnp.arange(kh)[None, :] - var[:, None] + NA_KH - 1
    b = rpb[:, ri[:, None, :, None], ci[None, :, None, :]].astype(F32)
    b = jnp.where(col_valid[None, None, :, None, :], b, -jnp.inf)
    b = b.reshape(N_PAIR, 2, kh, GRID_W, kh * GRID_W)
    return b.transpose(0, 2, 1, 3, 4).reshape(N_PAIR, kh, PAIR_W, kh * GRID_W)


def _mix_out_kernel(a_ref, u_ref, g_ref, x_ref, ws_ref, bs_ref, na_ref, nb_ref, wo_ref,
                    o_ref, sg_ref):
    tm = a_ref.shape[0]
    da = a_ref.shape[1]
    lane = lax.broadcasted_iota(jnp.int32, (1, PAIR_W), 1)
    head0 = lane < DH_B
    for c in range(tm // CHUNK):
        rsl = slice(c * CHUNK, (c + 1) * CHUNK)
        for p in range(H_B // 2):
            csl = slice(p * PAIR_W, (p + 1) * PAIR_W)
            m2 = _dot(ws_ref[p], g_ref[rsl, csl])
            mixed = jnp.where(head0, m2[:CHUNK], m2[CHUNK:]) + bs_ref[p]
            sg_ref[rsl, csl] = u_ref[rsl, csl] * mixed
    an = _rms(a_ref[...], na_ref[...]).astype(BF16)
    sn = _rms(sg_ref[...], nb_ref[...]).astype(BF16)
    o_ref[...] = x_ref[...] + _dot(an, wo_ref[:da, :]) + _dot(sn, wo_ref[da:, :])


def _mix_out(a, u, qkvg, x, ws, bs, na, nb, wo):
    n, d = x.shape
    da = a.shape[1]
    db = u.shape[1]
    tm = _pick(n, 512)
    return pl.pallas_call(
        _mix_out_kernel,
        grid=(n // tm,),
        in_specs=[
            pl.BlockSpec((tm, da), lambda i: (i, 0)),
            pl.BlockSpec((tm, db), lambda i: (i, 0)),
            pl.BlockSpec((None, tm, db), lambda i: (3, i, 0)),
            pl.BlockSpec((tm, d), lambda i: (i, 0)),
            pl.BlockSpec(ws.shape, lambda i: (0, 0, 0)),
            pl.BlockSpec(bs.shape, lambda i: (0, 0, 0)),
            pl.BlockSpec((1, da), lambda i: (0, 0)),
            pl.BlockSpec((1, db), lambda i: (0, 0)),
            pl.BlockSpec(wo.shape, lambda i: (0, 0)),
        ],
        out_specs=pl.BlockSpec((tm, d), lambda i: (i, 0)),
        out_shape=jax.ShapeDtypeStruct((n, d), F32),
        scratch_shapes=[pltpu.VMEM((tm, db), F32)],
        compiler_params=pltpu.CompilerParams(
            dimension_semantics=("arbitrary",),
            vmem_limit_bytes=VMEM_LIMIT),
        name="mix_out",
    )(a, u, qkvg, x, ws, bs, na, nb, wo)


def _layer(x, p):
    b, t, d = x.shape
    x = x.reshape(b * t, d)
    x1 = _ffn(x, p["ffn1_norm"], p["ffn1_w_in"], p["ffn1_w_out"])
    qkvg, u = _mix_in(x1, p["mix_norm"], p["w_in_mix"], p["q_norm"], p["k_norm"],
                      p["gate_norm"], p["bd"])
    a = _natten(qkvg, _attn_bias(p["attn_rpb"], t // GRID_W), b, t)
    x2 = _mix_out(a, u, qkvg, x1, p["w_spatial"], p["b_spatial"], p["out_norm_a"],
                  p["out_norm_b"], p["w_out_mix"])
    y = _ffn(x2, p["ffn2_norm"], p["ffn2_w_in"], p["ffn2_w_out"], p["final_norm"])
    return y.reshape(b, t, d)


def _prep(l, ffn1_norm, ffn1_w_in, ffn1_w_out, mix_norm, w_in_mix, q_norm, k_norm, attn_rpb,
          gate_norm, w_spatial, b_spatial, out_norm_a, out_norm_b, w_out_mix, ffn2_norm,
          ffn2_w_in, ffn2_w_out, final_norm):
    row = lambda v: v[l].astype(F32).reshape(1, -1)
    lane = jnp.arange(PAIR_W)
    bs = b_spatial[l].astype(F32).reshape(H_B // 2, 2, CHUNK)[:, lane // DH_B, :].transpose(0, 2, 1)
    blk = jnp.arange(MXU_DIM) // DH_A
    bd = jnp.where(blk[:, None] == blk[None, :], 1.0 / DH_A, 0.0).astype(BF16)
    return dict(
        ffn1_norm=row(ffn1_norm), ffn1_w_in=ffn1_w_in[l].astype(BF16),
        ffn1_w_out=ffn1_w_out[l].astype(BF16),
        mix_norm=row(mix_norm), w_in_mix=w_in_mix[l].astype(BF16),
        q_norm=jnp.tile(row(q_norm), (1, H_A)), k_norm=jnp.tile(row(k_norm), (1, H_A)),
        attn_rpb=attn_rpb[l], gate_norm=row(gate_norm),
        w_spatial=w_spatial[l].astype(BF16).reshape(H_B // 2, 2 * CHUNK, CHUNK),
        b_spatial=bs, out_norm_a=row(out_norm_a), out_norm_b=row(out_norm_b),
        w_out_mix=w_out_mix[l].astype(BF16),
        ffn2_norm=row(ffn2_norm), ffn2_w_in=ffn2_w_in[l].astype(BF16),
        ffn2_w_out=ffn2_w_out[l].astype(BF16), final_norm=row(final_norm), bd=bd,
    )


def kernel(x_prompt, x_sample, ffn1_norm, ffn1_w_in, ffn1_w_out, mix_norm, w_in_mix, q_norm, k_norm, attn_rpb, gate_norm, w_spatial, b_spatial, out_norm_a, out_norm_b, w_out_mix, ffn2_norm, ffn2_w_in, ffn2_w_out, final_norm):
    weights = (ffn1_norm, ffn1_w_in, ffn1_w_out, mix_norm, w_in_mix, q_norm, k_norm, attn_rpb,
               gate_norm, w_spatial, b_spatial, out_norm_a, out_norm_b, w_out_mix, ffn2_norm,
               ffn2_w_in, ffn2_w_out, final_norm)
    depth = ffn1_norm.shape[0]
    layers = [_prep(l, *weights) for l in range(depth)]

    def run(x):
        for p in layers:
            x = _layer(x, p)
        return x

    return (run(x_prompt), run(x_sample))
```

```python
import functools

import jax
import jax.numpy as jnp
from jax import lax
from jax.experimental import pallas as pl
from jax.experimental.pallas import tpu as pltpu

EPS = 1e-6
GRID_W = 64
NA_KH = 8
NA_KW = 16
H_A = 16
DH_A = 64
H_B = 16
DH_B = 64
CHUNK = 128
LANES = 128
PAIR_W = 2 * DH_A
N_PAIR = H_A // 2
MXU_DIM = 256
V7X_VMEM_BYTES = 64 * 1024 * 1024
VMEM_LIMIT = V7X_VMEM_BYTES - 6 * 1024 * 1024

F32 = jnp.float32
BF16 = jnp.bfloat16


def _rms(x, g):
    ms = jnp.mean(x * x, axis=-1, keepdims=True)
    return x * lax.rsqrt(ms + EPS) * g


def _gelu(x):
    return 0.5 * x * (1.0 + lax.erf(x * (2.0 ** -0.5)))


def _dot(a, b):
    return jnp.dot(a, b, preferred_element_type=F32)


def _pick(total, pref):
    b = min(total, pref)
    while total % b:
        b //= 2
    return b


def _ffn_kernel(x_ref, nrm_ref, wg_ref, wu_ref, wo_ref, *rest, final_norm):
    if final_norm:
        fn_ref, o_ref, h_ref = rest
    else:
        o_ref, h_ref = rest
    k = pl.program_id(1)

    @pl.when(k == 0)
    def _():
        x = x_ref[...]
        h_ref[...] = _rms(x, nrm_ref[...]).astype(BF16)
        o_ref[...] = x

    h = h_ref[...]
    g = _dot(h, wg_ref[...])
    u = _dot(h, wu_ref[...])
    a = ((0.5 * g) * jax.nn.sigmoid(g) * u).astype(BF16)
    o_ref[...] += _dot(a, wo_ref[...])

    if final_norm:
        @pl.when(k == pl.num_programs(1) - 1)
        def _():
            o_ref[...] = _rms(o_ref[...], fn_ref[...])


def _ffn(x, nrm, w_in, w_out, final_nrm=None):
    n, d = x.shape
    d_ff = w_out.shape[0]
    tm = _pick(n, 1024)
    tf = _pick(d_ff, 512)
    nk = d_ff // tf
    final = final_nrm is not None
    in_specs = [
        pl.BlockSpec((tm, d), lambda i, k: (i, 0)),
        pl.BlockSpec((1, d), lambda i, k: (0, 0)),
        pl.BlockSpec((d, tf), lambda i, k: (0, k)),
        pl.BlockSpec((d, tf), lambda i, k: (0, k + nk)),
        pl.BlockSpec((tf, d), lambda i, k: (k, 0)),
    ]
    args = [x, nrm, w_in, w_in, w_out]
    if final:
        in_specs.append(pl.BlockSpec((1, d), lambda i, k: (0, 0)))
        args.append(final_nrm)
    return pl.pallas_call(
        functools.partial(_ffn_kernel, final_norm=final),
        grid=(n // tm, nk),
        in_specs=in_specs,
        out_specs=pl.BlockSpec((tm, d), lambda i, k: (i, 0)),
        out_shape=jax.ShapeDtypeStruct((n, d), F32),
        scratch_shapes=[pltpu.VMEM((tm, d), BF16)],
        compiler_params=pltpu.CompilerParams(
            dimension_semantics=("arbitrary", "arbitrary"),
            vmem_limit_bytes=VMEM_LIMIT),
        name="ffn_final" if final else "ffn",
    )(*args)


def _head_mean_sq(z, bd):
    z2 = z * z
    hi = z2.astype(BF16)
    lo = (z2 - hi.astype(F32)).astype(BF16)
    parts = []
    for c in range(z.shape[1] // MXU_DIM):
        sl = slice(c * MXU_DIM, (c + 1) * MXU_DIM)
        parts.append(_dot(hi[:, sl], bd) + _dot(lo[:, sl], bd))
    return jnp.concatenate(parts, axis=1)


def _mix_in_kernel(x_ref, nrm_ref, w_ref, qn_ref, kn_ref, gn_ref, bd_ref,
                   qkvg_ref, u_ref, h_ref):
    da = u_ref.shape[1]
    h_ref[...] = _rms(x_ref[...], nrm_ref[...]).astype(BF16)

    def z(j):
        return _dot(h_ref[...], w_ref[:, j * da:(j + 1) * da])

    def head_norm(zj, gain):
        return zj * lax.rsqrt(_head_mean_sq(zj, bd_ref[...]) + EPS) * gain

    qkvg_ref[0] = (head_norm(z(0), qn_ref[...]) * (DH_A ** -0.5)).astype(BF16)
    qkvg_ref[1] = head_norm(z(1), kn_ref[...]).astype(BF16)
    qkvg_ref[2] = z(2).astype(BF16)
    u_ref[...] = _gelu(z(3))
    qkvg_ref[3] = _rms(_gelu(z(4)), gn_ref[...]).astype(BF16)


def _mix_in(x, nrm, w, qn, kn, gn, bd):
    n, d = x.shape
    da = H_A * DH_A
    tm = _pick(n, 512)
    return pl.pallas_call(
        _mix_in_kernel,
        grid=(n // tm,),
        in_specs=[
            pl.BlockSpec((tm, d), lambda i: (i, 0)),
            pl.BlockSpec((1, d), lambda i: (0, 0)),
            pl.BlockSpec(w.shape, lambda i: (0, 0), pipeline_mode=pl.Buffered(1)),
            pl.BlockSpec((1, da), lambda i: (0, 0)),
            pl.BlockSpec((1, da), lambda i: (0, 0)),
            pl.BlockSpec((1, da), lambda i: (0, 0)),
            pl.BlockSpec((MXU_DIM, MXU_DIM), lambda i: (0, 0)),
        ],
        out_specs=[
            pl.BlockSpec((4, tm, da), lambda i: (0, i, 0)),
            pl.BlockSpec((tm, da), lambda i: (i, 0)),
        ],
        out_shape=[
            jax.ShapeDtypeStruct((4, n, da), BF16),
            jax.ShapeDtypeStruct((n, da), F32),
        ],
        scratch_shapes=[pltpu.VMEM((tm, d), BF16)],
        compiler_params=pltpu.CompilerParams(
            dimension_semantics=("arbitrary",),
            vmem_limit_bytes=VMEM_LIMIT),
        name="mix_in",
    )(x, nrm, w, qn, kn, gn, bd)


def _natten_kernel(q_ref, k_ref, v_ref, bias_ref, o_ref, s_ref, *, rows, group):
    nkeys = NA_KH * GRID_W
    n_groups = rows // group
    lane = lax.broadcasted_iota(jnp.int32, (1, PAIR_W), 1)
    head0 = lane < DH_A

    def tokens(row, n):
        return pl.ds(pl.multiple_of(row * GRID_W, GRID_W), n)

    def row_start(r):
        return jnp.clip(r - NA_KH // 2, 0, rows - NA_KH)

    def scores(g, slot):
        for t in range(group):
            r = g * group + t
            rs = row_start(r)
            q = q_ref[tokens(r, GRID_W), :]
            zero = jnp.zeros_like(q)
            qm = jnp.concatenate([jnp.where(head0, q, zero), jnp.where(head0, zero, q)], axis=0)
            s = lax.dot_general(qm, k_ref[tokens(rs, nkeys), :], (((1,), (1,)), ((), ())),
                                preferred_element_type=F32)
            s_ref[slot, t] = s + bias_ref[r - rs]

    def attend(g, slot):
        for t in range(group):
            r = g * group + t
            s = s_ref[slot, t]
            m = jnp.max(s, axis=-1, keepdims=True)
            p = jnp.exp(s - m)
            l = jnp.sum(p, axis=-1, keepdims=True)
            o = _dot(p.astype(BF16), v_ref[tokens(row_start(r), nkeys), :]) / l
            o_ref[tokens(r, GRID_W), :] = jnp.where(head0, o[:GRID_W], o[GRID_W:])

    scores(0, 0)

    def body(j, carry):
        attend(2 * j, 0)
        scores(2 * j + 1, 1)
        attend(2 * j + 1, 1)
        scores(jnp.minimum(2 * j + 2, n_groups - 1), 0)
        return carry

    lax.fori_loop(0, n_groups // 2, body, 0)


def _natten(qkvg, bias, batch, seq):
    rows = seq // GRID_W
    group = 4
    assert seq % GRID_W == 0 and rows >= NA_KH, "token grid must hold a full attention window"
    assert rows % (2 * group) == 0
    da = H_A * DH_A

    def qkv_spec(slot):
        return pl.BlockSpec((None, seq, PAIR_W), lambda b, p: (slot, b, p))

    return pl.pallas_call(
        functools.partial(_natten_kernel, rows=rows, group=group),
        grid=(batch, N_PAIR),
        in_specs=[
            qkv_spec(0), qkv_spec(1), qkv_spec(2),
            pl.BlockSpec((None, NA_KH, PAIR_W, NA_KH * GRID_W), lambda b, p: (p, 0, 0, 0)),
        ],
        out_specs=pl.BlockSpec((seq, PAIR_W), lambda b, p: (b, p)),
        out_shape=jax.ShapeDtypeStruct((batch * seq, da), F32),
        scratch_shapes=[pltpu.VMEM((2, group, PAIR_W, NA_KH * GRID_W), F32)],
        compiler_params=pltpu.CompilerParams(
            dimension_semantics=("arbitrary", "arbitrary"),
            vmem_limit_bytes=VMEM_LIMIT),
        name="natten",
    )(qkvg, qkvg, qkvg, bias)


def _attn_bias_kernel(rpb_ref, o_ref):
    lane1 = lax.broadcasted_iota(jnp.int32, (1, LANES), 1)
    lane = lax.broadcasted_iota(jnp.int32, (GRID_W, LANES), 1)
    qcol = lax.broadcasted_iota(jnp.int32, (GRID_W, LANES), 0)
    kcol = lane & (GRID_W - 1)
    col_start = jnp.clip(qcol - NA_KW // 2, 0, GRID_W - NA_KW)
    masked = jnp.where((kcol >= col_start) & (kcol < col_start + NA_KW), 0.0, -jnp.inf)
    left = lane < GRID_W
    n_off = NA_KW - 1
    near = (lane1 <= n_off) | (lane1 >= LANES - n_off)
    for hh in range(2):
        r = rpb_ref[hh]
        far = jnp.where(lane1 < GRID_W, r[:, 2 * n_off:2 * n_off + 1], r[:, 0:1])
        g = jnp.where(near, pltpu.roll(r, LANES - n_off, 1), far)
        lo, hi = [], []
        for ri in range(2 * NA_KH - 1):
            gb = jnp.broadcast_to(g[ri:ri + 1, :], (GRID_W, LANES))
            t = pltpu.roll(gb, 0, 1, stride=1, stride_axis=0)
            lo.append(t)
            hi.append(pltpu.roll(t, GRID_W, 1))
        for var in range(NA_KH):
            for j in range(NA_KH // 2):
                ra = 2 * j - var + NA_KH - 1
                o_ref[var, hh * GRID_W:(hh + 1) * GRID_W, j * LANES:(j + 1) * LANES] = (
                    jnp.where(left, lo[ra], hi[ra + 1]) + masked)


def _attn_bias(rpb):
    h, nr, nc = rpb.shape
    rpb = jnp.pad(rpb.astype(F32), ((0, 0), (0, 2 * NA_KH - nr), (0, LANES - nc)))
    return pl.pallas_call(
        _attn_bias_kernel,
        grid=(N_PAIR,),
        in_specs=[pl.BlockSpec((2, 2 * NA_KH, LANES), lambda p: (p, 0, 0))],
        out_specs=pl.BlockSpec((None, NA_KH, PAIR_W, NA_KH * GRID_W), lambda p: (p, 0, 0, 0)),
        out_shape=jax.ShapeDtypeStruct((N_PAIR, NA_KH, PAIR_W, NA_KH * GRID_W), F32),
        compiler_params=pltpu.CompilerParams(dimension_semantics=("arbitrary",)),
        name="attn_bias",
    )(rpb)


def _mix_out_kernel(a_ref, u_ref, g_ref, x_ref, ws_ref, bs_ref, na_ref, nb_ref, wo_ref,
                    o_ref, sg_ref):
    tm = a_ref.shape[0]
    da = a_ref.shape[1]
    lane = lax.broadcasted_iota(jnp.int32, (1, PAIR_W), 1)
    head0 = lane < DH_B
    for c in range(tm // CHUNK):
        rsl = slice(c * CHUNK, (c + 1) * CHUNK)
        for p in range(H_B // 2):
            csl = slice(p * PAIR_W, (p + 1) * PAIR_W)
            m2 = _dot(ws_ref[p], g_ref[rsl, csl])
            mixed = jnp.where(head0, m2[:CHUNK], m2[CHUNK:]) + bs_ref[p]
            sg_ref[rsl, csl] = u_ref[rsl, csl] * mixed
    an = _rms(a_ref[...], na_ref[...]).astype(BF16)
    sn = _rms(sg_ref[...], nb_ref[...]).astype(BF16)
    o_ref[...] = x_ref[...] + _dot(an, wo_ref[:da, :]) + _dot(sn, wo_ref[da:, :])


def _mix_out(a, u, qkvg, x, ws, bs, na, nb, wo):
    n, d = x.shape
    da = a.shape[1]
    db = u.shape[1]
    tm = _pick(n, 512)
    return pl.pallas_call(
        _mix_out_kernel,
        grid=(n // tm,),
        in_specs=[
            pl.BlockSpec((tm, da), lambda i: (i, 0)),
            pl.BlockSpec((tm, db), lambda i: (i, 0)),
            pl.BlockSpec((None, tm, db), lambda i: (3, i, 0)),
            pl.BlockSpec((tm, d), lambda i: (i, 0)),
            pl.BlockSpec(ws.shape, lambda i: (0, 0, 0)),
            pl.BlockSpec(bs.shape, lambda i: (0, 0, 0)),
            pl.BlockSpec((1, da), lambda i: (0, 0)),
            pl.BlockSpec((1, db), lambda i: (0, 0)),
            pl.BlockSpec(wo.shape, lambda i: (0, 0)),
        ],
        out_specs=pl.BlockSpec((tm, d), lambda i: (i, 0)),
        out_shape=jax.ShapeDtypeStruct((n, d), F32),
        scratch_shapes=[pltpu.VMEM((tm, db), F32)],
        compiler_params=pltpu.CompilerParams(
            dimension_semantics=("arbitrary",),
            vmem_limit_bytes=VMEM_LIMIT),
        name="mix_out",
    )(a, u, qkvg, x, ws, bs, na, nb, wo)


def _layer(x, p):
    b, t, d = x.shape
    x = x.reshape(b * t, d)
    x1 = _ffn(x, p["ffn1_norm"], p["ffn1_w_in"], p["ffn1_w_out"])
    qkvg, u = _mix_in(x1, p["mix_norm"], p["w_in_mix"], p["q_norm"], p["k_norm"],
                      p["gate_norm"], p["bd"])
    a = _natten(qkvg, p["attn_bias"], b, t)
    x2 = _mix_out(a, u, qkvg, x1, p["w_spatial"], p["b_spatial"], p["out_norm_a"],
                  p["out_norm_b"], p["w_out_mix"])
    y = _ffn(x2, p["ffn2_norm"], p["ffn2_w_in"], p["ffn2_w_out"], p["final_norm"])
    return y.reshape(b, t, d)


def _prep(l, ffn1_norm, ffn1_w_in, ffn1_w_out, mix_norm, w_in_mix, q_norm, k_norm, attn_rpb,
          gate_norm, w_spatial, b_spatial, out_norm_a, out_norm_b, w_out_mix, ffn2_norm,
          ffn2_w_in, ffn2_w_out, final_norm):
    row = lambda v: v[l].astype(F32).reshape(1, -1)
    lane = jnp.arange(PAIR_W)
    bs = b_spatial[l].astype(F32).reshape(H_B // 2, 2, CHUNK)[:, lane // DH_B, :].transpose(0, 2, 1)
    blk = jnp.arange(MXU_DIM) // DH_A
    bd = jnp.where(blk[:, None] == blk[None, :], 1.0 / DH_A, 0.0).astype(BF16)
    return dict(
        ffn1_norm=row(ffn1_norm), ffn1_w_in=ffn1_w_in[l].astype(BF16),
        ffn1_w_out=ffn1_w_out[l].astype(BF16),
        mix_norm=row(mix_norm), w_in_mix=w_in_mix[l].astype(BF16),
        q_norm=jnp.tile(row(q_norm), (1, H_A)), k_norm=jnp.tile(row(k_norm), (1, H_A)),
        attn_bias=_attn_bias(attn_rpb[l]), gate_norm=row(gate_norm),
        w_spatial=w_spatial[l].astype(BF16).reshape(H_B // 2, 2 * CHUNK, CHUNK),
        b_spatial=bs, out_norm_a=row(out_norm_a), out_norm_b=row(out_norm_b),
        w_out_mix=w_out_mix[l].astype(BF16),
        ffn2_norm=row(ffn2_norm), ffn2_w_in=ffn2_w_in[l].astype(BF16),
        ffn2_w_out=ffn2_w_out[l].astype(BF16), final_norm=row(final_norm), bd=bd,
    )


def kernel(x_prompt, x_sample, ffn1_norm, ffn1_w_in, ffn1_w_out, mix_norm, w_in_mix, q_norm, k_norm, attn_rpb, gate_norm, w_spatial, b_spatial, out_norm_a, out_norm_b, w_out_mix, ffn2_norm, ffn2_w_in, ffn2_w_out, final_norm):
    weights = (ffn1_norm, ffn1_w_in, ffn1_w_out, mix_norm, w_in_mix, q_norm, k_norm, attn_rpb,
               gate_norm, w_spatial, b_spatial, out_norm_a, out_norm_b, w_out_mix, ffn2_norm,
               ffn2_w_in, ffn2_w_out, final_norm)
    depth = ffn1_norm.shape[0]
    layers = [_prep(l, *weights) for l in range(depth)]

    def run(x):
        for p in layers:
            x = _layer(x, p)
        return x

    return (run(x_prompt), run(x_sample))
```

```python
import functools

import jax
import jax.numpy as jnp
from jax import lax
from jax.experimental import pallas as pl
from jax.experimental.pallas import tpu as pltpu

EPS = 1e-6
GRID_W = 64
NA_KH = 8
NA_KW = 16
H_A = 16
DH_A = 64
H_B = 16
DH_B = 64
CHUNK = 128
LANES = 128
PAIR_W = 2 * DH_A
N_PAIR = H_A // 2
V7X_VMEM_BYTES = 64 * 1024 * 1024
VMEM_LIMIT = V7X_VMEM_BYTES - 6 * 1024 * 1024

F32 = jnp.float32
BF16 = jnp.bfloat16


def _rms(x, g):
    ms = jnp.mean(x * x, axis=-1, keepdims=True)
    return x * lax.rsqrt(ms + EPS) * g


def _gelu(x):
    return 0.5 * x * (1.0 + lax.erf(x * (2.0 ** -0.5)))


def _dot(a, b):
    return jnp.dot(a, b, preferred_element_type=F32)


def _pick(total, pref):
    b = min(total, pref)
    while total % b:
        b //= 2
    return b


def _ffn_kernel(x_ref, nrm_ref, wg_ref, wu_ref, wo_ref, *rest, final_norm):
    if final_norm:
        fn_ref, o_ref, h_ref = rest
    else:
        o_ref, h_ref = rest
    k = pl.program_id(1)

    @pl.when(k == 0)
    def _():
        x = x_ref[...]
        h_ref[...] = _rms(x, nrm_ref[...]).astype(BF16)
        o_ref[...] = x

    h = h_ref[...]
    g = _dot(h, wg_ref[...])
    u = _dot(h, wu_ref[...])
    a = ((0.5 * g) * jax.nn.sigmoid(g) * u).astype(BF16)
    o_ref[...] += _dot(a, wo_ref[...])

    if final_norm:
        @pl.when(k == pl.num_programs(1) - 1)
        def _():
            o_ref[...] = _rms(o_ref[...], fn_ref[...])


def _ffn(x, nrm, w_in, w_out, final_nrm=None):
    n, d = x.shape
    d_ff = w_out.shape[0]
    tm = _pick(n, 1024)
    tf = _pick(d_ff, 512)
    nk = d_ff // tf
    final = final_nrm is not None
    in_specs = [
        pl.BlockSpec((tm, d), lambda i, k: (i, 0)),
        pl.BlockSpec((1, d), lambda i, k: (0, 0)),
        pl.BlockSpec((d, tf), lambda i, k: (0, k)),
        pl.BlockSpec((d, tf), lambda i, k: (0, k + nk)),
        pl.BlockSpec((tf, d), lambda i, k: (k, 0)),
    ]
    args = [x, nrm, w_in, w_in, w_out]
    if final:
        in_specs.append(pl.BlockSpec((1, d), lambda i, k: (0, 0)))
        args.append(final_nrm)
    return pl.pallas_call(
        functools.partial(_ffn_kernel, final_norm=final),
        grid=(n // tm, nk),
        in_specs=in_specs,
        out_specs=pl.BlockSpec((tm, d), lambda i, k: (i, 0)),
        out_shape=jax.ShapeDtypeStruct((n, d), F32),
        scratch_shapes=[pltpu.VMEM((tm, d), BF16)],
        compiler_params=pltpu.CompilerParams(
            dimension_semantics=("arbitrary", "arbitrary"),
            vmem_limit_bytes=VMEM_LIMIT),
        name="ffn_final" if final else "ffn",
    )(*args)


def _head_mean_sq(z):
    lane = lax.broadcasted_iota(jnp.int32, (1, LANES), 1)
    head0 = lane < DH_A
    parts = []
    for c in range(z.shape[1] // LANES):
        blk = z[:, c * LANES:(c + 1) * LANES]
        sq = blk * blk
        a = jnp.sum(jnp.where(head0, sq, 0.0), axis=-1, keepdims=True)
        b = jnp.sum(jnp.where(head0, 0.0, sq), axis=-1, keepdims=True)
        parts.append(jnp.where(head0, a, b) * (1.0 / DH_A))
    return jnp.concatenate(parts, axis=1)


def _mix_in_kernel(x_ref, nrm_ref, w_ref, qn_ref, kn_ref, gn_ref,
                   qkvg_ref, u_ref, h_ref):
    da = u_ref.shape[1]
    h_ref[...] = _rms(x_ref[...], nrm_ref[...]).astype(BF16)

    def z(j):
        return _dot(h_ref[...], w_ref[:, j * da:(j + 1) * da])

    def head_norm(zj, gain):
        return zj * lax.rsqrt(_head_mean_sq(zj) + EPS) * gain

    qkvg_ref[3] = _rms(_gelu(z(4)), gn_ref[...]).astype(BF16)
    qkvg_ref[0] = (head_norm(z(0), qn_ref[...]) * (DH_A ** -0.5)).astype(BF16)
    qkvg_ref[1] = head_norm(z(1), kn_ref[...]).astype(BF16)
    u_ref[...] = _gelu(z(3))
    qkvg_ref[2] = z(2).astype(BF16)


def _mix_in(x, nrm, w, qn, kn, gn):
    n, d = x.shape
    da = H_A * DH_A
    tm = _pick(n, 512)
    return pl.pallas_call(
        _mix_in_kernel,
        grid=(n // tm,),
        in_specs=[
            pl.BlockSpec((tm, d), lambda i: (i, 0)),
            pl.BlockSpec((1, d), lambda i: (0, 0)),
            pl.BlockSpec(w.shape, lambda i: (0, 0), pipeline_mode=pl.Buffered(1)),
            pl.BlockSpec((1, da), lambda i: (0, 0)),
            pl.BlockSpec((1, da), lambda i: (0, 0)),
            pl.BlockSpec((1, da), lambda i: (0, 0)),
        ],
        out_specs=[
            pl.BlockSpec((4, tm, da), lambda i: (0, i, 0)),
            pl.BlockSpec((tm, da), lambda i: (i, 0)),
        ],
        out_shape=[
            jax.ShapeDtypeStruct((4, n, da), BF16),
            jax.ShapeDtypeStruct((n, da), F32),
        ],
        scratch_shapes=[pltpu.VMEM((tm, d), BF16)],
        compiler_params=pltpu.CompilerParams(
            dimension_semantics=("arbitrary",),
            vmem_limit_bytes=VMEM_LIMIT),
        name="mix_in",
    )(x, nrm, w, qn, kn, gn)


def _natten_kernel(q_ref, k_ref, v_ref, bias_ref, o_ref, s_ref, *, rows, group):
    nkeys = NA_KH * GRID_W
    n_groups = rows // group
    lane = lax.broadcasted_iota(jnp.int32, (1, PAIR_W), 1)
    head0 = lane < DH_A

    def tokens(row, n):
        return pl.ds(pl.multiple_of(row * GRID_W, GRID_W), n)

    def row_start(r):
        return jnp.clip(r - NA_KH // 2, 0, rows - NA_KH)

    def scores(g, slot):
        for t in range(group):
            r = g * group + t
            rs = row_start(r)
            q = q_ref[tokens(r, GRID_W), :]
            zero = jnp.zeros_like(q)
            qm = jnp.concatenate([jnp.where(head0, q, zero), jnp.where(head0, zero, q)], axis=0)
            s = lax.dot_general(qm, k_ref[tokens(rs, nkeys), :], (((1,), (1,)), ((), ())),
                                preferred_element_type=F32)
            s_ref[slot, t] = s + bias_ref[r - rs]

    def attend(g, slot):
        for t in range(group):
            r = g * group + t
            s = s_ref[slot, t]
            m = jnp.max(s, axis=-1, keepdims=True)
            p = jnp.exp(s - m)
            l = jnp.sum(p, axis=-1, keepdims=True)
            o = _dot(p.astype(BF16), v_ref[tokens(row_start(r), nkeys), :]) / l
            o_ref[tokens(r, GRID_W), :] = jnp.where(head0, o[:GRID_W], o[GRID_W:])

    scores(0, 0)

    def body(j, carry):
        attend(2 * j, 0)
        scores(2 * j + 1, 1)
        attend(2 * j + 1, 1)
        scores(jnp.minimum(2 * j + 2, n_groups - 1), 0)
        return carry

    lax.fori_loop(0, n_groups // 2, body, 0)


def _natten(qkvg, bias, batch, seq):
    rows = seq // GRID_W
    group = _pick(rows // 2, 8)
    assert seq % GRID_W == 0 and rows >= NA_KH, "token grid must hold a full attention window"
    assert rows % (2 * group) == 0
    da = H_A * DH_A

    def qkv_spec(slot):
        return pl.BlockSpec((None, seq, PAIR_W), lambda b, p: (slot, b, p))

    return pl.pallas_call(
        functools.partial(_natten_kernel, rows=rows, group=group),
        grid=(batch, N_PAIR),
        in_specs=[
            qkv_spec(0), qkv_spec(1), qkv_spec(2),
            pl.BlockSpec((None, NA_KH, PAIR_W, NA_KH * GRID_W), lambda b, p: (p, 0, 0, 0)),
        ],
        out_specs=pl.BlockSpec((seq, PAIR_W), lambda b, p: (b, p)),
        out_shape=jax.ShapeDtypeStruct((batch * seq, da), F32),
        scratch_shapes=[pltpu.VMEM((2, group, PAIR_W, NA_KH * GRID_W), F32)],
        compiler_params=pltpu.CompilerParams(
            dimension_semantics=("arbitrary", "arbitrary"),
            vmem_limit_bytes=VMEM_LIMIT),
        name="natten",
    )(qkvg, qkvg, qkvg, bias)


def _attn_bias_kernel(rpb_ref, o_ref):
    lane1 = lax.broadcasted_iota(jnp.int32, (1, LANES), 1)
    lane = lax.broadcasted_iota(jnp.int32, (GRID_W, LANES), 1)
    qcol = lax.broadcasted_iota(jnp.int32, (GRID_W, LANES), 0)
    kcol = lane & (GRID_W - 1)
    col_start = jnp.clip(qcol - NA_KW // 2, 0, GRID_W - NA_KW)
    masked = jnp.where((kcol >= col_start) & (kcol < col_start + NA_KW), 0.0, -jnp.inf)
    left = lane < GRID_W
    n_off = NA_KW - 1
    near = (lane1 <= n_off) | (lane1 >= LANES - n_off)
    for hh in range(2):
        r = rpb_ref[hh]
        far = jnp.where(lane1 < GRID_W, r[:, 2 * n_off:2 * n_off + 1], r[:, 0:1])
        g = jnp.where(near, pltpu.roll(r, LANES - n_off, 1), far)
        lo, hi = [], []
        for ri in range(2 * NA_KH - 1):
            gb = jnp.broadcast_to(g[ri:ri + 1, :], (GRID_W, LANES))
            t = pltpu.roll(gb, 0, 1, stride=1, stride_axis=0)
            lo.append(t)
            hi.append(pltpu.roll(t, GRID_W, 1))
        for var in range(NA_KH):
            for j in range(NA_KH // 2):
                ra = 2 * j - var + NA_KH - 1
                o_ref[var, hh * GRID_W:(hh + 1) * GRID_W, j * LANES:(j + 1) * LANES] = (
                    jnp.where(left, lo[ra], hi[ra + 1]) + masked)


def _attn_bias(rpb):
    h, nr, nc = rpb.shape
    rpb = jnp.pad(rpb.astype(F32), ((0, 0), (0, 2 * NA_KH - nr), (0, LANES - nc)))
    return pl.pallas_call(
        _attn_bias_kernel,
        grid=(N_PAIR,),
        in_specs=[pl.BlockSpec((2, 2 * NA_KH, LANES), lambda p: (p, 0, 0))],
        out_specs=pl.BlockSpec((None, NA_KH, PAIR_W, NA_KH * GRID_W), lambda p: (p, 0, 0, 0)),
        out_shape=jax.ShapeDtypeStruct((N_PAIR, NA_KH, PAIR_W, NA_KH * GRID_W), F32),
        compiler_params=pltpu.CompilerParams(dimension_semantics=("arbitrary",)),
        name="attn_bias",
    )(rpb)


def _mix_out_kernel(a_ref, u_ref, g_ref, x_ref, ws_ref, bs_ref, na_ref, nb_ref, wo_ref,
                    o_ref, sg_ref):
    tm = a_ref.shape[0]
    da = a_ref.shape[1]
    lane = lax.broadcasted_iota(jnp.int32, (1, PAIR_W), 1)
    head0 = lane < DH_B
    for c in range(tm // CHUNK):
        rsl = slice(c * CHUNK, (c + 1) * CHUNK)
        for p in range(H_B // 2):
            csl = slice(p * PAIR_W, (p + 1) * PAIR_W)
            m2 = _dot(ws_ref[p], g_ref[rsl, csl])
            mixed = jnp.where(head0, m2[:CHUNK], m2[CHUNK:]) + bs_ref[p]
            sg_ref[rsl, csl] = u_ref[rsl, csl] * mixed
    an = _rms(a_ref[...], na_ref[...]).astype(BF16)
    sn = _rms(sg_ref[...], nb_ref[...]).astype(BF16)
    o_ref[...] = x_ref[...] + _dot(an, wo_ref[:da, :]) + _dot(sn, wo_ref[da:, :])


def _mix_out(a, u, qkvg, x, ws, bs, na, nb, wo):
    n, d = x.shape
    da = a.shape[1]
    db = u.shape[1]
    tm = _pick(n, 512)
    return pl.pallas_call(
        _mix_out_kernel,
        grid=(n // tm,),
        in_specs=[
            pl.BlockSpec((tm, da), lambda i: (i, 0)),
            pl.BlockSpec((tm, db), lambda i: (i, 0)),
            pl.BlockSpec((None, tm, db), lambda i: (3, i, 0)),
            pl.BlockSpec((tm, d), lambda i: (i, 0)),
            pl.BlockSpec(ws.shape, lambda i: (0, 0, 0)),
            pl.BlockSpec(bs.shape, lambda i: (0, 0, 0)),
            pl.BlockSpec((1, da), lambda i: (0, 0)),
            pl.BlockSpec((1, db), lambda i: (0, 0)),
            pl.BlockSpec(wo.shape, lambda i: (0, 0)),
        ],
        out_specs=pl.BlockSpec((tm, d), lambda i: (i, 0)),
        out_shape=jax.ShapeDtypeStruct((n, d), F32),
        scratch_shapes=[pltpu.VMEM((tm, db), F32)],
        compiler_params=pltpu.CompilerParams(
            dimension_semantics=("arbitrary",),
            vmem_limit_bytes=VMEM_LIMIT),
        name="mix_out",
    )(a, u, qkvg, x, ws, bs, na, nb, wo)


def _layer(x, p):
    b, t, d = x.shape
    x = x.reshape(b * t, d)
    x1 = _ffn(x, p["ffn1_norm"], p["ffn1_w_in"], p["ffn1_w_out"])
    qkvg, u = _mix_in(x1, p["mix_norm"], p["w_in_mix"], p["q_norm"], p["k_norm"],
                      p["gate_norm"])
    a = _natten(qkvg, p["attn_bias"], b, t)
    x2 = _mix_out(a, u, qkvg, x1, p["w_spatial"], p["b_spatial"], p["out_norm_a"],
                  p["out_norm_b"], p["w_out_mix"])
    y = _ffn(x2, p["ffn2_norm"], p["ffn2_w_in"], p["ffn2_w_out"], p["final_norm"])
    return y.reshape(b, t, d)


def _prep(l, ffn1_norm, ffn1_w_in, ffn1_w_out, mix_norm, w_in_mix, q_norm, k_norm, attn_rpb,
          gate_norm, w_spatial, b_spatial, out_norm_a, out_norm_b, w_out_mix, ffn2_norm,
          ffn2_w_in, ffn2_w_out, final_norm):
    row = lambda v: v[l].astype(F32).reshape(1, -1)
    lane = jnp.arange(PAIR_W)
    bs = b_spatial[l].astype(F32).reshape(H_B // 2, 2, CHUNK)[:, lane // DH_B, :].transpose(0, 2, 1)
    return dict(
        ffn1_norm=row(ffn1_norm), ffn1_w_in=ffn1_w_in[l].astype(BF16),
        ffn1_w_out=ffn1_w_out[l].astype(BF16),
        mix_norm=row(mix_norm), w_in_mix=w_in_mix[l].astype(BF16),
        q_norm=jnp.tile(row(q_norm), (1, H_A)), k_norm=jnp.tile(row(k_norm), (1, H_A)),
        attn_bias=_attn_bias(attn_rpb[l]), gate_norm=row(gate_norm),
        w_spatial=w_spatial[l].astype(BF16).reshape(H_B // 2, 2 * CHUNK, CHUNK),
        b_spatial=bs, out_norm_a=row(out_norm_a), out_norm_b=row(out_norm_b),
        w_out_mix=w_out_mix[l].astype(BF16),
        ffn2_norm=row(ffn2_norm), ffn2_w_in=ffn2_w_in[l].astype(BF16),
        ffn2_w_out=ffn2_w_out[l].astype(BF16), final_norm=row(final_norm),
    )


def kernel(x_prompt, x_sample, ffn1_norm, ffn1_w_in, ffn1_w_out, mix_norm, w_in_mix, q_norm, k_norm, attn_rpb, gate_norm, w_spatial, b_spatial, out_norm_a, out_norm_b, w_out_mix, ffn2_norm, ffn2_w_in, ffn2_w_out, final_norm):
    weights = (ffn1_norm, ffn1_w_in, ffn1_w_out, mix_norm, w_in_mix, q_norm, k_norm, attn_rpb,
               gate_norm, w_spatial, b_spatial, out_norm_a, out_norm_b, w_out_mix, ffn2_norm,
               ffn2_w_in, ffn2_w_out, final_norm)
    depth = ffn1_norm.shape[0]
    layers = [_prep(l, *weights) for l in range(depth)]

    def run(x):
        for p in layers:
            x = _layer(x, p)
        return x

    return (run(x_prompt), run(x_sample))
```

```python
import functools

import jax
import jax.numpy as jnp
from jax import lax
from jax.experimental import pallas as pl
from jax.experimental.pallas import tpu as pltpu

EPS = 1e-6
GRID_W = 64
NA_KH = 8
NA_KW = 16
H_A = 16
DH_A = 64
H_B = 16
DH_B = 64
CHUNK = 128
LANES = 128
PAIR_W = 2 * DH_A
N_PAIR = H_A // 2
V7X_VMEM_BYTES = 64 * 1024 * 1024
VMEM_LIMIT = V7X_VMEM_BYTES - 6 * 1024 * 1024

F32 = jnp.float32
BF16 = jnp.bfloat16


def _rms(x, g):
    ms = jnp.mean(x * x, axis=-1, keepdims=True)
    return x * lax.rsqrt(ms + EPS) * g


def _gelu(x):
    return 0.5 * x * (1.0 + lax.erf(x * (2.0 ** -0.5)))


def _dot(a, b):
    return jnp.dot(a, b, preferred_element_type=F32)


def _pick(total, pref):
    b = min(total, pref)
    while total % b:
        b //= 2
    return b


def _ffn_kernel(x_ref, nrm_ref, wg_ref, wu_ref, wo_ref, *rest, final_norm):
    if final_norm:
        fn_ref, o_ref, h_ref = rest
    else:
        o_ref, h_ref = rest
    k = pl.program_id(1)

    @pl.when(k == 0)
    def _():
        x = x_ref[...]
        h_ref[...] = _rms(x, nrm_ref[...]).astype(BF16)
        o_ref[...] = x

    h = h_ref[...]
    g = _dot(h, wg_ref[...])
    u = _dot(h, wu_ref[...])
    a = ((0.5 * g) * jax.nn.sigmoid(g) * u).astype(BF16)
    o_ref[...] += _dot(a, wo_ref[...])

    if final_norm:
        @pl.when(k == pl.num_programs(1) - 1)
        def _():
            o_ref[...] = _rms(o_ref[...], fn_ref[...])


def _ffn(x, nrm, w_in, w_out, final_nrm=None):
    n, d = x.shape
    d_ff = w_out.shape[0]
    tm = _pick(n, 1024)
    tf = _pick(d_ff, 512)
    nk = d_ff // tf
    final = final_nrm is not None
    in_specs = [
        pl.BlockSpec((tm, d), lambda i, k: (i, 0)),
        pl.BlockSpec((1, d), lambda i, k: (0, 0)),
        pl.BlockSpec((d, tf), lambda i, k: (0, k)),
        pl.BlockSpec((d, tf), lambda i, k: (0, k + nk)),
        pl.BlockSpec((tf, d), lambda i, k: (k, 0)),
    ]
    args = [x, nrm, w_in, w_in, w_out]
    if final:
        in_specs.append(pl.BlockSpec((1, d), lambda i, k: (0, 0)))
        args.append(final_nrm)
    return pl.pallas_call(
        functools.partial(_ffn_kernel, final_norm=final),
        grid=(n // tm, nk),
        in_specs=in_specs,
        out_specs=pl.BlockSpec((tm, d), lambda i, k: (i, 0)),
        out_shape=jax.ShapeDtypeStruct((n, d), F32),
        scratch_shapes=[pltpu.VMEM((tm, d), BF16)],
        compiler_params=pltpu.CompilerParams(
            dimension_semantics=("arbitrary", "arbitrary"),
            vmem_limit_bytes=VMEM_LIMIT),
        name="ffn_final" if final else "ffn",
    )(*args)


def _head_mean_sq(z):
    lane = lax.broadcasted_iota(jnp.int32, (1, LANES), 1)
    head0 = lane < DH_A
    parts = []
    for c in range(z.shape[1] // LANES):
        blk = z[:, c * LANES:(c + 1) * LANES]
        sq = blk * blk
        a = jnp.sum(jnp.where(head0, sq, 0.0), axis=-1, keepdims=True)
        b = jnp.sum(jnp.where(head0, 0.0, sq), axis=-1, keepdims=True)
        parts.append(jnp.where(head0, a, b) * (1.0 / DH_A))
    return jnp.concatenate(parts, axis=1)


def _mix_in_kernel(x_ref, nrm_ref, w_ref, qn_ref, kn_ref, gn_ref,
                   qkvg_ref, u_ref, h_ref):
    da = u_ref.shape[1]
    h_ref[...] = _rms(x_ref[...], nrm_ref[...]).astype(BF16)

    def z(j):
        return _dot(h_ref[...], w_ref[:, j * da:(j + 1) * da])

    def head_norm(zj, gain):
        return zj * lax.rsqrt(_head_mean_sq(zj) + EPS) * gain

    def put(slot, val):
        val = val.astype(BF16)
        for p in range(da // PAIR_W):
            qkvg_ref[slot, p] = val[:, p * PAIR_W:(p + 1) * PAIR_W]

    put(3, _rms(_gelu(z(4)), gn_ref[...]))
    put(0, head_norm(z(0), qn_ref[...]) * (DH_A ** -0.5))
    put(1, head_norm(z(1), kn_ref[...]))
    u_ref[...] = _gelu(z(3))
    put(2, z(2))


def _mix_in(x, nrm, w, qn, kn, gn):
    n, d = x.shape
    da = H_A * DH_A
    tm = _pick(n, 512)
    return pl.pallas_call(
        _mix_in_kernel,
        grid=(n // tm,),
        in_specs=[
            pl.BlockSpec((tm, d), lambda i: (i, 0)),
            pl.BlockSpec((1, d), lambda i: (0, 0)),
            pl.BlockSpec(w.shape, lambda i: (0, 0), pipeline_mode=pl.Buffered(1)),
            pl.BlockSpec((1, da), lambda i: (0, 0)),
            pl.BlockSpec((1, da), lambda i: (0, 0)),
            pl.BlockSpec((1, da), lambda i: (0, 0)),
        ],
        out_specs=[
            pl.BlockSpec((4, da // PAIR_W, tm, PAIR_W), lambda i: (0, 0, i, 0)),
            pl.BlockSpec((tm, da), lambda i: (i, 0)),
        ],
        out_shape=[
            jax.ShapeDtypeStruct((4, da // PAIR_W, n, PAIR_W), BF16),
            jax.ShapeDtypeStruct((n, da), F32),
        ],
        scratch_shapes=[pltpu.VMEM((tm, d), BF16)],
        compiler_params=pltpu.CompilerParams(
            dimension_semantics=("arbitrary",),
            vmem_limit_bytes=VMEM_LIMIT),
        name="mix_in",
    )(x, nrm, w, qn, kn, gn)


def _natten_kernel(q_ref, k_ref, v_ref, bias_ref, o_ref, s_ref, *, rows, group):
    nkeys = NA_KH * GRID_W
    n_groups = rows // group
    lane = lax.broadcasted_iota(jnp.int32, (1, PAIR_W), 1)
    head0 = lane < DH_A

    def tokens(row, n):
        return pl.ds(pl.multiple_of(row * GRID_W, GRID_W), n)

    def row_start(r):
        return jnp.clip(r - NA_KH // 2, 0, rows - NA_KH)

    def scores(g, slot):
        for t in range(group):
            r = g * group + t
            rs = row_start(r)
            q = q_ref[tokens(r, GRID_W), :]
            zero = jnp.zeros_like(q)
            qm = jnp.concatenate([jnp.where(head0, q, zero), jnp.where(head0, zero, q)], axis=0)
            s = lax.dot_general(qm, k_ref[tokens(rs, nkeys), :], (((1,), (1,)), ((), ())),
                                preferred_element_type=F32)
            s_ref[slot, t] = s + bias_ref[r - rs]

    def attend(g, slot):
        for t in range(group):
            r = g * group + t
            s = s_ref[slot, t]
            m = jnp.max(s, axis=-1, keepdims=True)
            p = jnp.exp(s - m)
            l = jnp.sum(p, axis=-1, keepdims=True)
            o = _dot(p.astype(BF16), v_ref[tokens(row_start(r), nkeys), :]) / l
            o_ref[tokens(r, GRID_W), :] = jnp.where(head0, o[:GRID_W], o[GRID_W:])

    scores(0, 0)

    def body(j, carry):
        attend(2 * j, 0)
        scores(2 * j + 1, 1)
        attend(2 * j + 1, 1)
        scores(jnp.minimum(2 * j + 2, n_groups - 1), 0)
        return carry

    lax.fori_loop(0, n_groups // 2, body, 0)


def _natten(qkvg, bias, batch, seq):
    rows = seq // GRID_W
    group = _pick(rows // 2, 8)
    assert seq % GRID_W == 0 and rows >= NA_KH, "token grid must hold a full attention window"
    assert rows % (2 * group) == 0

    def qkv_spec(slot):
        return pl.BlockSpec((None, None, seq, PAIR_W), lambda b, p: (slot, p, b, 0))

    return pl.pallas_call(
        functools.partial(_natten_kernel, rows=rows, group=group),
        grid=(batch, N_PAIR),
        in_specs=[
            qkv_spec(0), qkv_spec(1), qkv_spec(2),
            pl.BlockSpec((None, NA_KH, PAIR_W, NA_KH * GRID_W), lambda b, p: (p, 0, 0, 0)),
        ],
        out_specs=pl.BlockSpec((None, seq, PAIR_W), lambda b, p: (p, b, 0)),
        out_shape=jax.ShapeDtypeStruct((N_PAIR, batch * seq, PAIR_W), F32),
        scratch_shapes=[pltpu.VMEM((2, group, PAIR_W, NA_KH * GRID_W), F32)],
        compiler_params=pltpu.CompilerParams(
            dimension_semantics=("arbitrary", "arbitrary"),
            vmem_limit_bytes=VMEM_LIMIT),
        name="natten",
    )(qkvg, qkvg, qkvg, bias)


def _attn_bias_kernel(rpb_ref, o_ref):
    lane1 = lax.broadcasted_iota(jnp.int32, (1, LANES), 1)
    lane = lax.broadcasted_iota(jnp.int32, (GRID_W, LANES), 1)
    qcol = lax.broadcasted_iota(jnp.int32, (GRID_W, LANES), 0)
    kcol = lane & (GRID_W - 1)
    col_start = jnp.clip(qcol - NA_KW // 2, 0, GRID_W - NA_KW)
    masked = jnp.where((kcol >= col_start) & (kcol < col_start + NA_KW), 0.0, -jnp.inf)
    left = lane < GRID_W
    n_off = NA_KW - 1
    near = (lane1 <= n_off) | (lane1 >= LANES - n_off)
    for hh in range(2):
        r = rpb_ref[hh]
        far = jnp.where(lane1 < GRID_W, r[:, 2 * n_off:2 * n_off + 1], r[:, 0:1])
        g = jnp.where(near, pltpu.roll(r, LANES - n_off, 1), far)
        lo, hi = [], []
        for ri in range(2 * NA_KH - 1):
            gb = jnp.broadcast_to(g[ri:ri + 1, :], (GRID_W, LANES))
            t = pltpu.roll(gb, 0, 1, stride=1, stride_axis=0)
            lo.append(t)
            hi.append(pltpu.roll(t, GRID_W, 1))
        for var in range(NA_KH):
            for j in range(NA_KH // 2):
                ra = 2 * j - var + NA_KH - 1
                o_ref[var, hh * GRID_W:(hh + 1) * GRID_W, j * LANES:(j + 1) * LANES] = (
                    jnp.where(left, lo[ra], hi[ra + 1]) + masked)


def _attn_bias(rpb):
    h, nr, nc = rpb.shape
    rpb = jnp.pad(rpb.astype(F32), ((0, 0), (0, 2 * NA_KH - nr), (0, LANES - nc)))
    return pl.pallas_call(
        _attn_bias_kernel,
        grid=(N_PAIR,),
        in_specs=[pl.BlockSpec((2, 2 * NA_KH, LANES), lambda p: (p, 0, 0))],
        out_specs=pl.BlockSpec((None, NA_KH, PAIR_W, NA_KH * GRID_W), lambda p: (p, 0, 0, 0)),
        out_shape=jax.ShapeDtypeStruct((N_PAIR, NA_KH, PAIR_W, NA_KH * GRID_W), F32),
        compiler_params=pltpu.CompilerParams(dimension_semantics=("arbitrary",)),
        name="attn_bias",
    )(rpb)


def _mix_out_kernel(a_ref, u_ref, g_ref, x_ref, ws_ref, bs_ref, na_ref, nb_ref, wo_ref,
                    o_ref, sg_ref):
    n_pair, tm, _ = a_ref.shape
    da = n_pair * PAIR_W
    lane = lax.broadcasted_iota(jnp.int32, (1, PAIR_W), 1)
    head0 = lane < DH_B
    n_chunk = tm // CHUNK
    for c0 in range(0, n_chunk, 2):
        chunks = range(c0, min(c0 + 2, n_chunk))
        for p in range(H_B // 2):
            csl = slice(p * PAIR_W, (p + 1) * PAIR_W)
            g = jnp.concatenate([g_ref[p, c * CHUNK:(c + 1) * CHUNK, :] for c in chunks], axis=1)
            zero = jnp.zeros_like(g)
            first = jnp.concatenate([head0] * len(chunks), axis=1)
            gg = jnp.concatenate([jnp.where(first, g, zero), jnp.where(first, zero, g)], axis=0)
            mixed = _dot(ws_ref[p], gg)
            for t, c in enumerate(chunks):
                rsl = slice(c * CHUNK, (c + 1) * CHUNK)
                sg_ref[rsl, csl] = u_ref[rsl, csl] * (mixed[:, t * PAIR_W:(t + 1) * PAIR_W] + bs_ref[p])
    a = jnp.concatenate([a_ref[p] for p in range(n_pair)], axis=1)
    an = _rms(a, na_ref[...]).astype(BF16)
    sn = _rms(sg_ref[...], nb_ref[...]).astype(BF16)
    o_ref[...] = x_ref[...] + _dot(an, wo_ref[:da, :]) + _dot(sn, wo_ref[da:, :])


def _mix_out(a, u, qkvg, x, ws, bs, na, nb, wo):
    n, d = x.shape
    n_pair = a.shape[0]
    da = n_pair * PAIR_W
    db = u.shape[1]
    tm = _pick(n, 512)
    return pl.pallas_call(
        _mix_out_kernel,
        grid=(n // tm,),
        in_specs=[
            pl.BlockSpec((n_pair, tm, PAIR_W), lambda i: (0, i, 0)),
            pl.BlockSpec((tm, db), lambda i: (i, 0)),
            pl.BlockSpec((None, db // PAIR_W, tm, PAIR_W), lambda i: (3, 0, i, 0)),
            pl.BlockSpec((tm, d), lambda i: (i, 0)),
            pl.BlockSpec(ws.shape, lambda i: (0, 0, 0)),
            pl.BlockSpec(bs.shape, lambda i: (0, 0, 0)),
            pl.BlockSpec((1, da), lambda i: (0, 0)),
            pl.BlockSpec((1, db), lambda i: (0, 0)),
            pl.BlockSpec(wo.shape, lambda i: (0, 0)),
        ],
        out_specs=pl.BlockSpec((tm, d), lambda i: (i, 0)),
        out_shape=jax.ShapeDtypeStruct((n, d), F32),
        scratch_shapes=[pltpu.VMEM((tm, db), F32)],
        compiler_params=pltpu.CompilerParams(
            dimension_semantics=("arbitrary",),
            vmem_limit_bytes=VMEM_LIMIT),
        name="mix_out",
    )(a, u, qkvg, x, ws, bs, na, nb, wo)


def _layer(x, p):
    b, t, d = x.shape
    x = x.reshape(b * t, d)
    x1 = _ffn(x, p["ffn1_norm"], p["ffn1_w_in"], p["ffn1_w_out"])
    qkvg, u = _mix_in(x1, p["mix_norm"], p["w_in_mix"], p["q_norm"], p["k_norm"],
                      p["gate_norm"])
    a = _natten(qkvg, p["attn_bias"], b, t)
    x2 = _mix_out(a, u, qkvg, x1, p["w_spatial"], p["b_spatial"], p["out_norm_a"],
                  p["out_norm_b"], p["w_out_mix"])
    y = _ffn(x2, p["ffn2_norm"], p["ffn2_w_in"], p["ffn2_w_out"], p["final_norm"])
    return y.reshape(b, t, d)


def _prep(l, ffn1_norm, ffn1_w_in, ffn1_w_out, mix_norm, w_in_mix, q_norm, k_norm, attn_rpb,
          gate_norm, w_spatial, b_spatial, out_norm_a, out_norm_b, w_out_mix, ffn2_norm,
          ffn2_w_in, ffn2_w_out, final_norm):
    row = lambda v: v[l].astype(F32).reshape(1, -1)
    lane = jnp.arange(PAIR_W)
    bs = b_spatial[l].astype(F32).reshape(H_B // 2, 2, CHUNK)[:, lane // DH_B, :].transpose(0, 2, 1)
    return dict(
        ffn1_norm=row(ffn1_norm), ffn1_w_in=ffn1_w_in[l].astype(BF16),
        ffn1_w_out=ffn1_w_out[l].astype(BF16),
        mix_norm=row(mix_norm), w_in_mix=w_in_mix[l].astype(BF16),
        q_norm=jnp.tile(row(q_norm), (1, H_A)), k_norm=jnp.tile(row(k_norm), (1, H_A)),
        attn_bias=_attn_bias(attn_rpb[l]), gate_norm=row(gate_norm),
        w_spatial=w_spatial[l].astype(BF16).reshape(H_B // 2, 2, CHUNK, CHUNK).transpose(
            0, 2, 1, 3).reshape(H_B // 2, CHUNK, 2 * CHUNK),
        b_spatial=bs, out_norm_a=row(out_norm_a), out_norm_b=row(out_norm_b),
        w_out_mix=w_out_mix[l].astype(BF16),
        ffn2_norm=row(ffn2_norm), ffn2_w_in=ffn2_w_in[l].astype(BF16),
        ffn2_w_out=ffn2_w_out[l].astype(BF16), final_norm=row(final_norm),
    )


def kernel(x_prompt, x_sample, ffn1_norm, ffn1_w_in, ffn1_w_out, mix_norm, w_in_mix, q_norm, k_norm, attn_rpb, gate_norm, w_spatial, b_spatial, out_norm_a, out_norm_b, w_out_mix, ffn2_norm, ffn2_w_in, ffn2_w_out, final_norm):
    weights = (ffn1_norm, ffn1_w_in, ffn1_w_out, mix_norm, w_in_mix, q_norm, k_norm, attn_rpb,
               gate_norm, w_spatial, b_spatial, out_norm_a, out_norm_b, w_out_mix, ffn2_norm,
               ffn2_w_in, ffn2_w_out, final_norm)
    depth = ffn1_norm.shape[0]
    layers = [_prep(l, *weights) for l in range(depth)]

    def run(x):
        for p in layers:
            x = _layer(x, p)
        return x

    return (run(x_prompt), run(x_sample))
```

```python
import functools

import jax
import jax.numpy as jnp
from jax import lax
from jax.experimental import pallas as pl
from jax.experimental.pallas import tpu as pltpu

EPS = 1e-6
GRID_W = 64
NA_KH = 8
NA_KW = 16
H_A = 16
DH_A = 64
H_B = 16
DH_B = 64
CHUNK = 128
LANES = 128
PAIR_W = 2 * DH_A
N_PAIR = H_A // 2
FFN_CHUNK = 512
V7X_VMEM_BYTES = 64 * 1024 * 1024
V7X_VMEM_RESERVED_BYTES = 6 * 1024 * 1024
VMEM_LIMIT = V7X_VMEM_BYTES - V7X_VMEM_RESERVED_BYTES

F32 = jnp.float32
BF16 = jnp.bfloat16


def _rms(x, g):
    ms = jnp.mean(x * x, axis=-1, keepdims=True)
    return x * lax.rsqrt(ms + EPS) * g


def _gelu(x):
    return 0.5 * x * (1.0 + lax.erf(x * (2.0 ** -0.5)))


def _dot(a, b):
    return jnp.dot(a, b, preferred_element_type=F32)


def _pick(total, pref):
    b = min(total, pref)
    while total % b:
        b //= 2
    return b


def _ffn_kernel(x_ref, nrm_ref, wg_ref, wu_ref, wo_ref, *rest, final_norm):
    if final_norm:
        fn_ref, o_ref, h_ref = rest
    else:
        o_ref, h_ref = rest
    k = pl.program_id(1)

    @pl.when(k == 0)
    def _():
        x = x_ref[...]
        h_ref[...] = _rms(x, nrm_ref[...]).astype(BF16)
        o_ref[...] = x

    h = h_ref[...]
    g = _dot(h, wg_ref[...])
    u = _dot(h, wu_ref[...])
    a = ((0.5 * g) * jax.nn.sigmoid(g) * u).astype(BF16)
    o_ref[...] += _dot(a, wo_ref[...])

    if final_norm:
        @pl.when(k == pl.num_programs(1) - 1)
        def _():
            o_ref[...] = _rms(o_ref[...], fn_ref[...])


def _chunk_cols(w, width):
    d, c = w.shape
    return w.reshape(d, c // width, width).transpose(1, 0, 2)


def _ffn(x, nrm, w_in, w_out, final_nrm=None):
    n, d = x.shape
    nk, tf = w_in.shape[0] // 2, w_in.shape[2]
    tm = _pick(n, 1024)
    final = final_nrm is not None
    in_specs = [
        pl.BlockSpec((tm, d), lambda i, k: (i, 0)),
        pl.BlockSpec((1, d), lambda i, k: (0, 0)),
        pl.BlockSpec((None, d, tf), lambda i, k: (k, 0, 0)),
        pl.BlockSpec((None, d, tf), lambda i, k: (k + nk, 0, 0)),
        pl.BlockSpec((tf, d), lambda i, k: (k, 0)),
    ]
    args = [x, nrm, w_in, w_in, w_out]
    if final:
        in_specs.append(pl.BlockSpec((1, d), lambda i, k: (0, 0)))
        args.append(final_nrm)
    return pl.pallas_call(
        functools.partial(_ffn_kernel, final_norm=final),
        grid=(n // tm, nk),
        in_specs=in_specs,
        out_specs=pl.BlockSpec((tm, d), lambda i, k: (i, 0)),
        out_shape=jax.ShapeDtypeStruct((n, d), F32),
        scratch_shapes=[pltpu.VMEM((tm, d), BF16)],
        compiler_params=pltpu.CompilerParams(
            dimension_semantics=("arbitrary", "arbitrary"),
            vmem_limit_bytes=VMEM_LIMIT),
        name="ffn_final" if final else "ffn",
    )(*args)


def _head_mean_sq(z):
    lane = lax.broadcasted_iota(jnp.int32, (1, LANES), 1)
    head0 = lane < DH_A
    parts = []
    for c in range(z.shape[1] // LANES):
        blk = z[:, c * LANES:(c + 1) * LANES]
        sq = blk * blk
        a = jnp.sum(jnp.where(head0, sq, 0.0), axis=-1, keepdims=True)
        b = jnp.sum(jnp.where(head0, 0.0, sq), axis=-1, keepdims=True)
        parts.append(jnp.where(head0, a, b) * (1.0 / DH_A))
    return jnp.concatenate(parts, axis=1)


def _mix_in_kernel(x_ref, nrm_ref, w_ref, qn_ref, kn_ref, gn_ref,
                   qkvg_ref, u_ref, h_ref):
    da = u_ref.shape[1]
    h_ref[...] = _rms(x_ref[...], nrm_ref[...]).astype(BF16)

    def z(j):
        return _dot(h_ref[...], w_ref[:, j * da:(j + 1) * da])

    def head_norm(zj, gain):
        return zj * lax.rsqrt(_head_mean_sq(zj) + EPS) * gain

    def put(slot, val):
        val = val.astype(BF16)
        for p in range(da // PAIR_W):
            qkvg_ref[slot, p] = val[:, p * PAIR_W:(p + 1) * PAIR_W]

    put(3, _rms(_gelu(z(4)), gn_ref[...]))
    put(0, head_norm(z(0), qn_ref[...]) * (DH_A ** -0.5))
    put(1, head_norm(z(1), kn_ref[...]))
    u_ref[...] = _gelu(z(3))
    put(2, z(2))


def _mix_in(x, nrm, w, qn, kn, gn):
    n, d = x.shape
    da = H_A * DH_A
    tm = _pick(n, 512)
    return pl.pallas_call(
        _mix_in_kernel,
        grid=(n // tm,),
        in_specs=[
            pl.BlockSpec((tm, d), lambda i: (i, 0)),
            pl.BlockSpec((1, d), lambda i: (0, 0)),
            pl.BlockSpec(w.shape, lambda i: (0, 0), pipeline_mode=pl.Buffered(1)),
            pl.BlockSpec((1, da), lambda i: (0, 0)),
            pl.BlockSpec((1, da), lambda i: (0, 0)),
            pl.BlockSpec((1, da), lambda i: (0, 0)),
        ],
        out_specs=[
            pl.BlockSpec((4, da // PAIR_W, tm, PAIR_W), lambda i: (0, 0, i, 0)),
            pl.BlockSpec((tm, da), lambda i: (i, 0)),
        ],
        out_shape=[
            jax.ShapeDtypeStruct((4, da // PAIR_W, n, PAIR_W), BF16),
            jax.ShapeDtypeStruct((n, da), F32),
        ],
        scratch_shapes=[pltpu.VMEM((tm, d), BF16)],
        compiler_params=pltpu.CompilerParams(
            dimension_semantics=("arbitrary",),
            vmem_limit_bytes=VMEM_LIMIT),
        name="mix_in",
    )(x, nrm, w, qn, kn, gn)


def _natten_kernel(q_ref, k_ref, v_ref, bias_ref, o_ref, s_ref, *, rows, group):
    nkeys = NA_KH * GRID_W
    n_groups = rows // group
    lane = lax.broadcasted_iota(jnp.int32, (1, PAIR_W), 1)
    head0 = lane < DH_A

    def tokens(row, n):
        return pl.ds(pl.multiple_of(row * GRID_W, GRID_W), n)

    def row_start(r):
        return jnp.clip(r - NA_KH // 2, 0, rows - NA_KH)

    def scores(g, slot):
        for t in range(group):
            r = g * group + t
            rs = row_start(r)
            q = q_ref[tokens(r, GRID_W), :]
            zero = jnp.zeros_like(q)
            qm = jnp.concatenate([jnp.where(head0, q, zero), jnp.where(head0, zero, q)], axis=0)
            s = lax.dot_general(qm, k_ref[tokens(rs, nkeys), :], (((1,), (1,)), ((), ())),
                                preferred_element_type=F32)
            s_ref[slot, t] = s + bias_ref[r - rs]

    def attend(g, slot):
        for t in range(group):
            r = g * group + t
            s = s_ref[slot, t]
            m = jnp.max(s, axis=-1, keepdims=True)
            p = jnp.exp(s - m)
            l = jnp.sum(p, axis=-1, keepdims=True)
            o = _dot(p.astype(BF16), v_ref[tokens(row_start(r), nkeys), :]) / l
            o_ref[tokens(r, GRID_W), :] = jnp.where(head0, o[:GRID_W], o[GRID_W:])

    scores(0, 0)

    def body(j, carry):
        attend(2 * j, 0)
        scores(2 * j + 1, 1)
        attend(2 * j + 1, 1)
        scores(jnp.minimum(2 * j + 2, n_groups - 1), 0)
        return carry

    lax.fori_loop(0, n_groups // 2, body, 0)


def _natten(qkvg, bias, batch, seq):
    rows = seq // GRID_W
    group = _pick(rows // 2, 8)
    assert seq % GRID_W == 0 and rows >= NA_KH, "token grid must hold a full attention window"
    assert rows % (2 * group) == 0

    def qkv_spec(slot):
        return pl.BlockSpec((None, None, seq, PAIR_W), lambda b, p: (slot, p, b, 0))

    return pl.pallas_call(
        functools.partial(_natten_kernel, rows=rows, group=group),
        grid=(batch, N_PAIR),
        in_specs=[
            qkv_spec(0), qkv_spec(1), qkv_spec(2),
            pl.BlockSpec((None, NA_KH, PAIR_W, NA_KH * GRID_W), lambda b, p: (p, 0, 0, 0)),
        ],
        out_specs=pl.BlockSpec((None, seq, PAIR_W), lambda b, p: (p, b, 0)),
        out_shape=jax.ShapeDtypeStruct((N_PAIR, batch * seq, PAIR_W), F32),
        scratch_shapes=[pltpu.VMEM((2, group, PAIR_W, NA_KH * GRID_W), F32)],
        compiler_params=pltpu.CompilerParams(
            dimension_semantics=("arbitrary", "arbitrary"),
            vmem_limit_bytes=VMEM_LIMIT),
        name="natten",
    )(qkvg, qkvg, qkvg, bias)


def _attn_bias_kernel(rpb_ref, o_ref):
    lane1 = lax.broadcasted_iota(jnp.int32, (1, LANES), 1)
    lane = lax.broadcasted_iota(jnp.int32, (GRID_W, LANES), 1)
    qcol = lax.broadcasted_iota(jnp.int32, (GRID_W, LANES), 0)
    kcol = lane & (GRID_W - 1)
    col_start = jnp.clip(qcol - NA_KW // 2, 0, GRID_W - NA_KW)
    masked = jnp.where((kcol >= col_start) & (kcol < col_start + NA_KW), 0.0, -jnp.inf)
    left = lane < GRID_W
    n_off = NA_KW - 1
    near = (lane1 <= n_off) | (lane1 >= LANES - n_off)
    for hh in range(2):
        r = rpb_ref[hh]
        far = jnp.where(lane1 < GRID_W, r[:, 2 * n_off:2 * n_off + 1], r[:, 0:1])
        g = jnp.where(near, pltpu.roll(r, LANES - n_off, 1), far)
        lo, hi = [], []
        for ri in range(2 * NA_KH - 1):
            gb = jnp.broadcast_to(g[ri:ri + 1, :], (GRID_W, LANES))
            t = pltpu.roll(gb, 0, 1, stride=1, stride_axis=0)
            lo.append(t)
            hi.append(pltpu.roll(t, GRID_W, 1))
        for var in range(NA_KH):
            for j in range(NA_KH // 2):
                ra = 2 * j - var + NA_KH - 1
                o_ref[var, hh * GRID_W:(hh + 1) * GRID_W, j * LANES:(j + 1) * LANES] = (
                    jnp.where(left, lo[ra], hi[ra + 1]) + masked)


def _attn_bias(rpb):
    h, nr, nc = rpb.shape
    rpb = jnp.pad(rpb.astype(F32), ((0, 0), (0, 2 * NA_KH - nr), (0, LANES - nc)))
    return pl.pallas_call(
        _attn_bias_kernel,
        grid=(N_PAIR,),
        in_specs=[pl.BlockSpec((2, 2 * NA_KH, LANES), lambda p: (p, 0, 0))],
        out_specs=pl.BlockSpec((None, NA_KH, PAIR_W, NA_KH * GRID_W), lambda p: (p, 0, 0, 0)),
        out_shape=jax.ShapeDtypeStruct((N_PAIR, NA_KH, PAIR_W, NA_KH * GRID_W), F32),
        compiler_params=pltpu.CompilerParams(dimension_semantics=("arbitrary",)),
        name="attn_bias",
    )(rpb)


def _mix_out_kernel(a_ref, u_ref, g_ref, x_ref, ws_ref, bs_ref, na_ref, nb_ref, wo_ref,
                    o_ref, sg_ref):
    n_pair, tm, _ = a_ref.shape
    da = n_pair * PAIR_W
    lane = lax.broadcasted_iota(jnp.int32, (1, PAIR_W), 1)
    head0 = lane < DH_B
    n_chunk = tm // CHUNK
    for c0 in range(0, n_chunk, 2):
        chunks = range(c0, min(c0 + 2, n_chunk))
        for p in range(H_B // 2):
            csl = slice(p * PAIR_W, (p + 1) * PAIR_W)
            g = jnp.concatenate([g_ref[p, c * CHUNK:(c + 1) * CHUNK, :] for c in chunks], axis=1)
            zero = jnp.zeros_like(g)
            first = jnp.concatenate([head0] * len(chunks), axis=1)
            gg = jnp.concatenate([jnp.where(first, g, zero), jnp.where(first, zero, g)], axis=0)
            mixed = _dot(ws_ref[p], gg)
            for t, c in enumerate(chunks):
                rsl = slice(c * CHUNK, (c + 1) * CHUNK)
                sg_ref[rsl, csl] = u_ref[rsl, csl] * (mixed[:, t * PAIR_W:(t + 1) * PAIR_W] + bs_ref[p])
    a = jnp.concatenate([a_ref[p] for p in range(n_pair)], axis=1)
    an = _rms(a, na_ref[...]).astype(BF16)
    sn = _rms(sg_ref[...], nb_ref[...]).astype(BF16)
    o_ref[...] = x_ref[...] + _dot(an, wo_ref[:da, :]) + _dot(sn, wo_ref[da:, :])


def _mix_out(a, u, qkvg, x, ws, bs, na, nb, wo):
    n, d = x.shape
    n_pair = a.shape[0]
    da = n_pair * PAIR_W
    db = u.shape[1]
    tm = _pick(n, 512)
    return pl.pallas_call(
        _mix_out_kernel,
        grid=(n // tm,),
        in_specs=[
            pl.BlockSpec((n_pair, tm, PAIR_W), lambda i: (0, i, 0)),
            pl.BlockSpec((tm, db), lambda i: (i, 0)),
            pl.BlockSpec((None, db // PAIR_W, tm, PAIR_W), lambda i: (3, 0, i, 0)),
            pl.BlockSpec((tm, d), lambda i: (i, 0)),
            pl.BlockSpec(ws.shape, lambda i: (0, 0, 0)),
            pl.BlockSpec(bs.shape, lambda i: (0, 0, 0)),
            pl.BlockSpec((1, da), lambda i: (0, 0)),
            pl.BlockSpec((1, db), lambda i: (0, 0)),
            pl.BlockSpec(wo.shape, lambda i: (0, 0)),
        ],
        out_specs=pl.BlockSpec((tm, d), lambda i: (i, 0)),
        out_shape=jax.ShapeDtypeStruct((n, d), F32),
        scratch_shapes=[pltpu.VMEM((tm, db), F32)],
        compiler_params=pltpu.CompilerParams(
            dimension_semantics=("arbitrary",),
            vmem_limit_bytes=VMEM_LIMIT),
        name="mix_out",
    )(a, u, qkvg, x, ws, bs, na, nb, wo)


def _layer(x, p):
    b, t, d = x.shape
    x = x.reshape(b * t, d)
    x1 = _ffn(x, p["ffn1_norm"], p["ffn1_w_in"], p["ffn1_w_out"])
    qkvg, u = _mix_in(x1, p["mix_norm"], p["w_in_mix"], p["q_norm"], p["k_norm"],
                      p["gate_norm"])
    a = _natten(qkvg, p["attn_bias"], b, t)
    x2 = _mix_out(a, u, qkvg, x1, p["w_spatial"], p["b_spatial"], p["out_norm_a"],
                  p["out_norm_b"], p["w_out_mix"])
    y = _ffn(x2, p["ffn2_norm"], p["ffn2_w_in"], p["ffn2_w_out"], p["final_norm"])
    return y.reshape(b, t, d)


def _prep(l, ffn1_norm, ffn1_w_in, ffn1_w_out, mix_norm, w_in_mix, q_norm, k_norm, attn_rpb,
          gate_norm, w_spatial, b_spatial, out_norm_a, out_norm_b, w_out_mix, ffn2_norm,
          ffn2_w_in, ffn2_w_out, final_norm):
    row = lambda v: v[l].astype(F32).reshape(1, -1)
    ffn_w_in = lambda w: _chunk_cols(w[l].astype(BF16), _pick(w.shape[2] // 2, FFN_CHUNK))
    lane = jnp.arange(PAIR_W)
    bs = b_spatial[l].astype(F32).reshape(H_B // 2, 2, CHUNK)[:, lane // DH_B, :].transpose(0, 2, 1)
    return dict(
        ffn1_norm=row(ffn1_norm), ffn1_w_in=ffn_w_in(ffn1_w_in),
        ffn1_w_out=ffn1_w_out[l].astype(BF16),
        mix_norm=row(mix_norm), w_in_mix=w_in_mix[l].astype(BF16),
        q_norm=jnp.tile(row(q_norm), (1, H_A)), k_norm=jnp.tile(row(k_norm), (1, H_A)),
        attn_bias=_attn_bias(attn_rpb[l]), gate_norm=row(gate_norm),
        w_spatial=w_spatial[l].astype(BF16).reshape(H_B // 2, 2, CHUNK, CHUNK).transpose(
            0, 2, 1, 3).reshape(H_B // 2, CHUNK, 2 * CHUNK),
        b_spatial=bs, out_norm_a=row(out_norm_a), out_norm_b=row(out_norm_b),
        w_out_mix=w_out_mix[l].astype(BF16),
        ffn2_norm=row(ffn2_norm), ffn2_w_in=ffn_w_in(ffn2_w_in),
        ffn2_w_out=ffn2_w_out[l].astype(BF16), final_norm=row(final_norm),
    )


def kernel(x_prompt, x_sample, ffn1_norm, ffn1_w_in, ffn1_w_out, mix_norm, w_in_mix, q_norm, k_norm, attn_rpb, gate_norm, w_spatial, b_spatial, out_norm_a, out_norm_b, w_out_mix, ffn2_norm, ffn2_w_in, ffn2_w_out, final_norm):
    weights = (ffn1_norm, ffn1_w_in, ffn1_w_out, mix_norm, w_in_mix, q_norm, k_norm, attn_rpb,
               gate_norm, w_spatial, b_spatial, out_norm_a, out_norm_b, w_out_mix, ffn2_norm,
               ffn2_w_in, ffn2_w_out, final_norm)
    depth = ffn1_norm.shape[0]
    layers = [_prep(l, *weights) for l in range(depth)]

    def run(x):
        for p in layers:
            x = _layer(x, p)
        return x

    return (run(x_prompt), run(x_sample))
```

```python
import functools

import jax
import jax.numpy as jnp
from jax import lax
from jax.experimental import pallas as pl
from jax.experimental.pallas import tpu as pltpu

EPS = 1e-6
GRID_W = 64
NA_KH = 8
NA_KW = 16
H_A = 16
DH_A = 64
H_B = 16
DH_B = 64
CHUNK = 128
LANES = 128
PAIR_W = 2 * DH_A
N_PAIR = H_A // 2
V7X_VMEM_BYTES = 64 * 1024 * 1024
VMEM_LIMIT = V7X_VMEM_BYTES - 6 * 1024 * 1024

F32 = jnp.float32
BF16 = jnp.bfloat16


def _rms(x, g):
    ms = jnp.mean(x * x, axis=-1, keepdims=True)
    return x * lax.rsqrt(ms + EPS) * g


def _gelu(x):
    return 0.5 * x * (1.0 + lax.erf(x * (2.0 ** -0.5)))


def _dot(a, b):
    return jnp.dot(a, b, preferred_element_type=F32)


def _pick(total, pref):
    b = min(total, pref)
    while total % b:
        b //= 2
    return b


def _ffn_kernel(x_ref, nrm_ref, wg_ref, wu_ref, wo_ref, *rest, final_norm, n_chunks):
    if final_norm:
        fn_ref, o_ref, h_ref = rest
    else:
        o_ref, h_ref = rest
    k = pl.program_id(1)

    def step(first, final):
        if first:
            h_ref[...] = _rms(x_ref[...], nrm_ref[...]).astype(BF16)
        h = h_ref[...]
        g = _dot(h, wg_ref[...])
        u = _dot(h, wu_ref[...])
        a = ((0.5 * g) * jax.nn.sigmoid(g) * u).astype(BF16)
        acc = (x_ref[...] if first else o_ref[...]) + _dot(a, wo_ref[...])
        o_ref[...] = _rms(acc, fn_ref[...]) if final else acc

    if n_chunks == 1:
        step(True, final_norm)
    elif final_norm:
        pl.when(k == 0)(lambda: step(True, False))
        pl.when((k > 0) & (k < n_chunks - 1))(lambda: step(False, False))
        pl.when(k == n_chunks - 1)(lambda: step(False, True))
    else:
        pl.when(k == 0)(lambda: step(True, False))
        pl.when(k > 0)(lambda: step(False, False))


def _ffn(x, nrm, w_in, w_out, final_nrm=None):
    n, d = x.shape
    d_ff = w_out.shape[0]
    tm = _pick(n, 1024)
    tf = _pick(d_ff, 512)
    nk = d_ff // tf
    final = final_nrm is not None
    in_specs = [
        pl.BlockSpec((tm, d), lambda i, k: (i, 0)),
        pl.BlockSpec((1, d), lambda i, k: (0, 0)),
        pl.BlockSpec((d, tf), lambda i, k: (0, k)),
        pl.BlockSpec((d, tf), lambda i, k: (0, k + nk)),
        pl.BlockSpec((tf, d), lambda i, k: (k, 0)),
    ]
    args = [x, nrm, w_in, w_in, w_out]
    if final:
        in_specs.append(pl.BlockSpec((1, d), lambda i, k: (0, 0)))
        args.append(final_nrm)
    return pl.pallas_call(
        functools.partial(_ffn_kernel, final_norm=final, n_chunks=nk),
        grid=(n // tm, nk),
        in_specs=in_specs,
        out_specs=pl.BlockSpec((tm, d), lambda i, k: (i, 0)),
        out_shape=jax.ShapeDtypeStruct((n, d), F32),
        scratch_shapes=[pltpu.VMEM((tm, d), BF16)],
        compiler_params=pltpu.CompilerParams(
            dimension_semantics=("arbitrary", "arbitrary"),
            vmem_limit_bytes=VMEM_LIMIT),
        name="ffn_final" if final else "ffn",
    )(*args)


def _head_mean_sq(z):
    lane = lax.broadcasted_iota(jnp.int32, (1, LANES), 1)
    head0 = lane < DH_A
    parts = []
    for c in range(z.shape[1] // LANES):
        blk = z[:, c * LANES:(c + 1) * LANES]
        sq = blk * blk
        a = jnp.sum(jnp.where(head0, sq, 0.0), axis=-1, keepdims=True)
        b = jnp.sum(jnp.where(head0, 0.0, sq), axis=-1, keepdims=True)
        parts.append(jnp.where(head0, a, b) * (1.0 / DH_A))
    return jnp.concatenate(parts, axis=1)


def _mix_in_kernel(x_ref, nrm_ref, w_ref, qn_ref, kn_ref, gn_ref,
                   qkvg_ref, u_ref, h_ref):
    da = u_ref.shape[1]
    h_ref[...] = _rms(x_ref[...], nrm_ref[...]).astype(BF16)

    def z(j):
        return _dot(h_ref[...], w_ref[:, j * da:(j + 1) * da])

    def head_norm(zj, gain):
        return zj * lax.rsqrt(_head_mean_sq(zj) + EPS) * gain

    def put(slot, val):
        val = val.astype(BF16)
        for p in range(da // PAIR_W):
            qkvg_ref[slot, p] = val[:, p * PAIR_W:(p + 1) * PAIR_W]

    put(3, _rms(_gelu(z(4)), gn_ref[...]))
    put(0, head_norm(z(0), qn_ref[...]) * (DH_A ** -0.5))
    put(1, head_norm(z(1), kn_ref[...]))
    u_ref[...] = _gelu(z(3))
    put(2, z(2))


def _mix_in(x, nrm, w, qn, kn, gn):
    n, d = x.shape
    da = H_A * DH_A
    tm = _pick(n, 512)
    return pl.pallas_call(
        _mix_in_kernel,
        grid=(n // tm,),
        in_specs=[
            pl.BlockSpec((tm, d), lambda i: (i, 0)),
            pl.BlockSpec((1, d), lambda i: (0, 0)),
            pl.BlockSpec(w.shape, lambda i: (0, 0), pipeline_mode=pl.Buffered(1)),
            pl.BlockSpec((1, da), lambda i: (0, 0)),
            pl.BlockSpec((1, da), lambda i: (0, 0)),
            pl.BlockSpec((1, da), lambda i: (0, 0)),
        ],
        out_specs=[
            pl.BlockSpec((4, da // PAIR_W, tm, PAIR_W), lambda i: (0, 0, i, 0)),
            pl.BlockSpec((tm, da), lambda i: (i, 0)),
        ],
        out_shape=[
            jax.ShapeDtypeStruct((4, da // PAIR_W, n, PAIR_W), BF16),
            jax.ShapeDtypeStruct((n, da), F32),
        ],
        scratch_shapes=[pltpu.VMEM((tm, d), BF16)],
        compiler_params=pltpu.CompilerParams(
            dimension_semantics=("arbitrary",),
            vmem_limit_bytes=VMEM_LIMIT),
        name="mix_in",
    )(x, nrm, w, qn, kn, gn)


def _natten_kernel(q_ref, k_ref, v_ref, bias_ref, o_ref, s_ref, *, rows, group):
    nkeys = NA_KH * GRID_W
    n_groups = rows // group
    lane = lax.broadcasted_iota(jnp.int32, (1, PAIR_W), 1)
    head0 = lane < DH_A

    def tokens(row, n):
        return pl.ds(pl.multiple_of(row * GRID_W, GRID_W), n)

    def row_start(r):
        return jnp.clip(r - NA_KH // 2, 0, rows - NA_KH)

    def scores(g, slot):
        for t in range(group):
            r = g * group + t
            rs = row_start(r)
            q = q_ref[tokens(r, GRID_W), :]
            zero = jnp.zeros_like(q)
            qm = jnp.concatenate([jnp.where(head0, q, zero), jnp.where(head0, zero, q)], axis=0)
            s = lax.dot_general(qm, k_ref[tokens(rs, nkeys), :], (((1,), (1,)), ((), ())),
                                preferred_element_type=F32)
            s_ref[slot, t] = s + bias_ref[r - rs]

    def attend(g, slot):
        for t in range(group):
            r = g * group + t
            s = s_ref[slot, t]
            m = jnp.max(s, axis=-1, keepdims=True)
            p = jnp.exp(s - m)
            l = jnp.sum(p, axis=-1, keepdims=True)
            o = _dot(p.astype(BF16), v_ref[tokens(row_start(r), nkeys), :]) / l
            o_ref[tokens(r, GRID_W), :] = jnp.where(head0, o[:GRID_W], o[GRID_W:])

    scores(0, 0)

    def body(j, carry):
        attend(2 * j, 0)
        scores(2 * j + 1, 1)
        attend(2 * j + 1, 1)
        scores(jnp.minimum(2 * j + 2, n_groups - 1), 0)
        return carry

    lax.fori_loop(0, n_groups // 2, body, 0)


def _natten(qkvg, bias, batch, seq):
    rows = seq // GRID_W
    group = _pick(rows // 2, 8)
    assert seq % GRID_W == 0 and rows >= NA_KH, "token grid must hold a full attention window"
    assert rows % (2 * group) == 0

    def qkv_spec(slot):
        return pl.BlockSpec((None, None, seq, PAIR_W), lambda b, p: (slot, p, b, 0))

    return pl.pallas_call(
        functools.partial(_natten_kernel, rows=rows, group=group),
        grid=(batch, N_PAIR),
        in_specs=[
            qkv_spec(0), qkv_spec(1), qkv_spec(2),
            pl.BlockSpec((None, NA_KH, PAIR_W, NA_KH * GRID_W), lambda b, p: (p, 0, 0, 0)),
        ],
        out_specs=pl.BlockSpec((None, seq, PAIR_W), lambda b, p: (p, b, 0)),
        out_shape=jax.ShapeDtypeStruct((N_PAIR, batch * seq, PAIR_W), F32),
        scratch_shapes=[pltpu.VMEM((2, group, PAIR_W, NA_KH * GRID_W), F32)],
        compiler_params=pltpu.CompilerParams(
            dimension_semantics=("arbitrary", "arbitrary"),
            vmem_limit_bytes=VMEM_LIMIT),
        name="natten",
    )(qkvg, qkvg, qkvg, bias)


def _attn_bias_kernel(rpb_ref, o_ref):
    lane1 = lax.broadcasted_iota(jnp.int32, (1, LANES), 1)
    lane = lax.broadcasted_iota(jnp.int32, (GRID_W, LANES), 1)
    qcol = lax.broadcasted_iota(jnp.int32, (GRID_W, LANES), 0)
    kcol = lane & (GRID_W - 1)
    col_start = jnp.clip(qcol - NA_KW // 2, 0, GRID_W - NA_KW)
    masked = jnp.where((kcol >= col_start) & (kcol < col_start + NA_KW), 0.0, -jnp.inf)
    left = lane < GRID_W
    n_off = NA_KW - 1
    near = (lane1 <= n_off) | (lane1 >= LANES - n_off)
    for hh in range(2):
        r = rpb_ref[hh]
        far = jnp.where(lane1 < GRID_W, r[:, 2 * n_off:2 * n_off + 1], r[:, 0:1])
        g = jnp.where(near, pltpu.roll(r, LANES - n_off, 1), far)
        lo, hi = [], []
        for ri in range(2 * NA_KH - 1):
            gb = jnp.broadcast_to(g[ri:ri + 1, :], (GRID_W, LANES))
            t = pltpu.roll(gb, 0, 1, stride=1, stride_axis=0)
            lo.append(t)
            hi.append(pltpu.roll(t, GRID_W, 1))
        for var in range(NA_KH):
            for j in range(NA_KH // 2):
                ra = 2 * j - var + NA_KH - 1
                o_ref[var, hh * GRID_W:(hh + 1) * GRID_W, j * LANES:(j + 1) * LANES] = (
                    jnp.where(left, lo[ra], hi[ra + 1]) + masked)


def _attn_bias(rpb):
    h, nr, nc = rpb.shape
    rpb = jnp.pad(rpb.astype(F32), ((0, 0), (0, 2 * NA_KH - nr), (0, LANES - nc)))
    return pl.pallas_call(
        _attn_bias_kernel,
        grid=(N_PAIR,),
        in_specs=[pl.BlockSpec((2, 2 * NA_KH, LANES), lambda p: (p, 0, 0))],
        out_specs=pl.BlockSpec((None, NA_KH, PAIR_W, NA_KH * GRID_W), lambda p: (p, 0, 0, 0)),
        out_shape=jax.ShapeDtypeStruct((N_PAIR, NA_KH, PAIR_W, NA_KH * GRID_W), F32),
        compiler_params=pltpu.CompilerParams(dimension_semantics=("arbitrary",)),
        name="attn_bias",
    )(rpb)


def _mix_out_kernel(a_ref, u_ref, g_ref, x_ref, ws_ref, bs_ref, na_ref, nb_ref, wo_ref,
                    o_ref, sg_ref):
    n_pair, tm, _ = a_ref.shape
    da = n_pair * PAIR_W
    lane = lax.broadcasted_iota(jnp.int32, (1, PAIR_W), 1)
    head0 = lane < DH_B
    n_chunk = tm // CHUNK
    for c0 in range(0, n_chunk, 2):
        chunks = range(c0, min(c0 + 2, n_chunk))
        for p in range(H_B // 2):
            csl = slice(p * PAIR_W, (p + 1) * PAIR_W)
            g = jnp.concatenate([g_ref[p, c * CHUNK:(c + 1) * CHUNK, :] for c in chunks], axis=1)
            zero = jnp.zeros_like(g)
            first = jnp.concatenate([head0] * len(chunks), axis=1)
            gg = jnp.concatenate([jnp.where(first, g, zero), jnp.where(first, zero, g)], axis=0)
            mixed = _dot(ws_ref[p], gg)
            for t, c in enumerate(chunks):
                rsl = slice(c * CHUNK, (c + 1) * CHUNK)
                sg_ref[rsl, csl] = u_ref[rsl, csl] * (mixed[:, t * PAIR_W:(t + 1) * PAIR_W] + bs_ref[p])
    a = jnp.concatenate([a_ref[p] for p in range(n_pair)], axis=1)
    an = _rms(a, na_ref[...]).astype(BF16)
    sn = _rms(sg_ref[...], nb_ref[...]).astype(BF16)
    o_ref[...] = x_ref[...] + _dot(an, wo_ref[:da, :]) + _dot(sn, wo_ref[da:, :])


def _mix_out(a, u, qkvg, x, ws, bs, na, nb, wo):
    n, d = x.shape
    n_pair = a.shape[0]
    da = n_pair * PAIR_W
    db = u.shape[1]
    tm = _pick(n, 512)
    return pl.pallas_call(
        _mix_out_kernel,
        grid=(n // tm,),
        in_specs=[
            pl.BlockSpec((n_pair, tm, PAIR_W), lambda i: (0, i, 0)),
            pl.BlockSpec((tm, db), lambda i: (i, 0)),
            pl.BlockSpec((None, db // PAIR_W, tm, PAIR_W), lambda i: (3, 0, i, 0)),
            pl.BlockSpec((tm, d), lambda i: (i, 0)),
            pl.BlockSpec(ws.shape, lambda i: (0, 0, 0)),
            pl.BlockSpec(bs.shape, lambda i: (0, 0, 0)),
            pl.BlockSpec((1, da), lambda i: (0, 0)),
            pl.BlockSpec((1, db), lambda i: (0, 0)),
            pl.BlockSpec(wo.shape, lambda i: (0, 0)),
        ],
        out_specs=pl.BlockSpec((tm, d), lambda i: (i, 0)),
        out_shape=jax.ShapeDtypeStruct((n, d), F32),
        scratch_shapes=[pltpu.VMEM((tm, db), F32)],
        compiler_params=pltpu.CompilerParams(
            dimension_semantics=("arbitrary",),
            vmem_limit_bytes=VMEM_LIMIT),
        name="mix_out",
    )(a, u, qkvg, x, ws, bs, na, nb, wo)


def _layer(x, p):
    b, t, d = x.shape
    x = x.reshape(b * t, d)
    x1 = _ffn(x, p["ffn1_norm"], p["ffn1_w_in"], p["ffn1_w_out"])
    qkvg, u = _mix_in(x1, p["mix_norm"], p["w_in_mix"], p["q_norm"], p["k_norm"],
                      p["gate_norm"])
    a = _natten(qkvg, p["attn_bias"], b, t)
    x2 = _mix_out(a, u, qkvg, x1, p["w_spatial"], p["b_spatial"], p["out_norm_a"],
                  p["out_norm_b"], p["w_out_mix"])
    y = _ffn(x2, p["ffn2_norm"], p["ffn2_w_in"], p["ffn2_w_out"], p["final_norm"])
    return y.reshape(b, t, d)


def _prep(l, ffn1_norm, ffn1_w_in, ffn1_w_out, mix_norm, w_in_mix, q_norm, k_norm, attn_rpb,
          gate_norm, w_spatial, b_spatial, out_norm_a, out_norm_b, w_out_mix, ffn2_norm,
          ffn2_w_in, ffn2_w_out, final_norm):
    row = lambda v: v[l].astype(F32).reshape(1, -1)
    lane = jnp.arange(PAIR_W)
    bs = b_spatial[l].astype(F32).reshape(H_B // 2, 2, CHUNK)[:, lane // DH_B, :].transpose(0, 2, 1)
    return dict(
        ffn1_norm=row(ffn1_norm), ffn1_w_in=ffn1_w_in[l].astype(BF16),
        ffn1_w_out=ffn1_w_out[l].astype(BF16),
        mix_norm=row(mix_norm), w_in_mix=w_in_mix[l].astype(BF16),
        q_norm=jnp.tile(row(q_norm), (1, H_A)), k_norm=jnp.tile(row(k_norm), (1, H_A)),
        attn_bias=_attn_bias(attn_rpb[l]), gate_norm=row(gate_norm),
        w_spatial=w_spatial[l].astype(BF16).reshape(H_B // 2, 2, CHUNK, CHUNK).transpose(
            0, 2, 1, 3).reshape(H_B // 2, CHUNK, 2 * CHUNK),
        b_spatial=bs, out_norm_a=row(out_norm_a), out_norm_b=row(out_norm_b),
        w_out_mix=w_out_mix[l].astype(BF16),
        ffn2_norm=row(ffn2_norm), ffn2_w_in=ffn2_w_in[l].astype(BF16),
        ffn2_w_out=ffn2_w_out[l].astype(BF16), final_norm=row(final_norm),
    )


def kernel(x_prompt, x_sample, ffn1_norm, ffn1_w_in, ffn1_w_out, mix_norm, w_in_mix, q_norm, k_norm, attn_rpb, gate_norm, w_spatial, b_spatial, out_norm_a, out_norm_b, w_out_mix, ffn2_norm, ffn2_w_in, ffn2_w_out, final_norm):
    weights = (ffn1_norm, ffn1_w_in, ffn1_w_out, mix_norm, w_in_mix, q_norm, k_norm, attn_rpb,
               gate_norm, w_spatial, b_spatial, out_norm_a, out_norm_b, w_out_mix, ffn2_norm,
               ffn2_w_in, ffn2_w_out, final_norm)
    depth = ffn1_norm.shape[0]
    layers = [_prep(l, *weights) for l in range(depth)]

    def run(x):
        for p in layers:
            x = _layer(x, p)
        return x

    return (run(x_prompt), run(x_sample))
```

```python
import functools

import jax
import jax.numpy as jnp
from jax import lax
from jax.experimental import pallas as pl
from jax.experimental.pallas import tpu as pltpu

EPS = 1e-6
LOG2E = 1.4426950408889634
GRID_W = 64
NA_KH = 8
NA_KW = 16
H_A = 16
DH_A = 64
H_B = 16
DH_B = 64
CHUNK = 128
LANES = 128
PAIR_W = 2 * DH_A
N_PAIR = H_A // 2
V7X_VMEM_BYTES = 64 * 1024 * 1024
VMEM_LIMIT = V7X_VMEM_BYTES - 6 * 1024 * 1024

F32 = jnp.float32
BF16 = jnp.bfloat16


def _rms(x, g):
    ms = jnp.mean(x * x, axis=-1, keepdims=True)
    return x * lax.rsqrt(ms + EPS) * g


def _gelu(x):
    return 0.5 * x * (1.0 + lax.erf(x * (2.0 ** -0.5)))


def _dot(a, b):
    return jnp.dot(a, b, preferred_element_type=F32)


def _pick(total, pref):
    b = min(total, pref)
    while total % b:
        b //= 2
    return b


def _ffn_kernel(x_ref, nrm_ref, wg_ref, wu_ref, wo_ref, *rest, final_norm, n_chunks):
    if final_norm:
        fn_ref, o_ref, h_ref = rest
    else:
        o_ref, h_ref = rest
    k = pl.program_id(1)

    def step(first, final):
        if first:
            h_ref[...] = _rms(x_ref[...], nrm_ref[...]).astype(BF16)
        h = h_ref[...]
        g = _dot(h, wg_ref[...])
        u = _dot(h, wu_ref[...])
        a = ((0.5 * g) * jax.nn.sigmoid(g) * u).astype(BF16)
        acc = (x_ref[...] if first else o_ref[...]) + _dot(a, wo_ref[...])
        o_ref[...] = _rms(acc, fn_ref[...]) if final else acc

    if n_chunks == 1:
        step(True, final_norm)
    elif final_norm:
        pl.when(k == 0)(lambda: step(True, False))
        pl.when((k > 0) & (k < n_chunks - 1))(lambda: step(False, False))
        pl.when(k == n_chunks - 1)(lambda: step(False, True))
    else:
        pl.when(k == 0)(lambda: step(True, False))
        pl.when(k > 0)(lambda: step(False, False))


def _ffn(x, nrm, w_in, w_out, final_nrm=None):
    n, d = x.shape
    d_ff = w_out.shape[0]
    tm = _pick(n, 1024)
    tf = _pick(d_ff, 512)
    nk = d_ff // tf
    final = final_nrm is not None
    in_specs = [
        pl.BlockSpec((tm, d), lambda i, k: (i, 0)),
        pl.BlockSpec((1, d), lambda i, k: (0, 0)),
        pl.BlockSpec((d, tf), lambda i, k: (0, k)),
        pl.BlockSpec((d, tf), lambda i, k: (0, k + nk)),
        pl.BlockSpec((tf, d), lambda i, k: (k, 0)),
    ]
    args = [x, nrm, w_in, w_in, w_out]
    if final:
        in_specs.append(pl.BlockSpec((1, d), lambda i, k: (0, 0)))
        args.append(final_nrm)
    return pl.pallas_call(
        functools.partial(_ffn_kernel, final_norm=final, n_chunks=nk),
        grid=(n // tm, nk),
        in_specs=in_specs,
        out_specs=pl.BlockSpec((tm, d), lambda i, k: (i, 0)),
        out_shape=jax.ShapeDtypeStruct((n, d), F32),
        scratch_shapes=[pltpu.VMEM((tm, d), BF16)],
        compiler_params=pltpu.CompilerParams(
            dimension_semantics=("arbitrary", "arbitrary"),
            vmem_limit_bytes=VMEM_LIMIT),
        name="ffn_final" if final else "ffn",
    )(*args)


def _head_mean_sq(z):
    lane = lax.broadcasted_iota(jnp.int32, (1, LANES), 1)
    head0 = lane < DH_A
    parts = []
    for c in range(z.shape[1] // LANES):
        blk = z[:, c * LANES:(c + 1) * LANES]
        sq = blk * blk
        a = jnp.sum(jnp.where(head0, sq, 0.0), axis=-1, keepdims=True)
        b = jnp.sum(jnp.where(head0, 0.0, sq), axis=-1, keepdims=True)
        parts.append(jnp.where(head0, a, b) * (1.0 / DH_A))
    return jnp.concatenate(parts, axis=1)


def _mix_in_kernel(x_ref, nrm_ref, w_ref, qn_ref, kn_ref, gn_ref,
                   qkvg_ref, u_ref, h_ref):
    da = u_ref.shape[1]
    h_ref[...] = _rms(x_ref[...], nrm_ref[...]).astype(BF16)

    def z(j):
        return _dot(h_ref[...], w_ref[:, j * da:(j + 1) * da])

    def head_norm(zj, gain):
        return zj * lax.rsqrt(_head_mean_sq(zj) + EPS) * gain

    def put(slot, val):
        val = val.astype(BF16)
        for p in range(da // PAIR_W):
            qkvg_ref[slot, p] = val[:, p * PAIR_W:(p + 1) * PAIR_W]

    put(3, _rms(_gelu(z(4)), gn_ref[...]))
    put(0, head_norm(z(0), qn_ref[...]) * (DH_A ** -0.5 * LOG2E))
    put(1, head_norm(z(1), kn_ref[...]))
    u_ref[...] = _gelu(z(3))
    put(2, z(2))


def _mix_in(x, nrm, w, qn, kn, gn):
    n, d = x.shape
    da = H_A * DH_A
    tm = _pick(n, 512)
    return pl.pallas_call(
        _mix_in_kernel,
        grid=(n // tm,),
        in_specs=[
            pl.BlockSpec((tm, d), lambda i: (i, 0)),
            pl.BlockSpec((1, d), lambda i: (0, 0)),
            pl.BlockSpec(w.shape, lambda i: (0, 0), pipeline_mode=pl.Buffered(1)),
            pl.BlockSpec((1, da), lambda i: (0, 0)),
            pl.BlockSpec((1, da), lambda i: (0, 0)),
            pl.BlockSpec((1, da), lambda i: (0, 0)),
        ],
        out_specs=[
            pl.BlockSpec((4, da // PAIR_W, tm, PAIR_W), lambda i: (0, 0, i, 0)),
            pl.BlockSpec((tm, da), lambda i: (i, 0)),
        ],
        out_shape=[
            jax.ShapeDtypeStruct((4, da // PAIR_W, n, PAIR_W), BF16),
            jax.ShapeDtypeStruct((n, da), F32),
        ],
        scratch_shapes=[pltpu.VMEM((tm, d), BF16)],
        compiler_params=pltpu.CompilerParams(
            dimension_semantics=("arbitrary",),
            vmem_limit_bytes=VMEM_LIMIT),
        name="mix_in",
    )(x, nrm, w, qn, kn, gn)


def _natten_kernel(q_ref, k_ref, v_ref, bias_ref, o_ref, s_ref, *, rows, group):
    nkeys = NA_KH * GRID_W
    n_groups = rows // group
    lane = lax.broadcasted_iota(jnp.int32, (1, PAIR_W), 1)
    head0 = lane < DH_A

    def tokens(row, n):
        return pl.ds(pl.multiple_of(row * GRID_W, GRID_W), n)

    def row_start(r):
        return jnp.clip(r - NA_KH // 2, 0, rows - NA_KH)

    def scores(g, slot):
        for t in range(group):
            r = g * group + t
            rs = row_start(r)
            q = q_ref[tokens(r, GRID_W), :]
            zero = jnp.zeros_like(q)
            qm = jnp.concatenate([jnp.where(head0, q, zero), jnp.where(head0, zero, q)], axis=0)
            s = lax.dot_general(qm, k_ref[tokens(rs, nkeys), :], (((1,), (1,)), ((), ())),
                                preferred_element_type=F32)
            s_ref[slot, t] = s + bias_ref[r - rs]

    def attend(g, slot):
        for t in range(group):
            r = g * group + t
            s = s_ref[slot, t]
            m = jnp.max(s, axis=-1, keepdims=True)
            p = jnp.exp2(s - m)
            l = jnp.sum(p, axis=-1, keepdims=True)
            o = _dot(p.astype(BF16), v_ref[tokens(row_start(r), nkeys), :]) / l
            o_ref[tokens(r, GRID_W), :] = jnp.where(head0, o[:GRID_W], o[GRID_W:])

    scores(0, 0)

    def body(j, carry):
        attend(2 * j, 0)
        scores(2 * j + 1, 1)
        attend(2 * j + 1, 1)
        scores(jnp.minimum(2 * j + 2, n_groups - 1), 0)
        return carry

    lax.fori_loop(0, n_groups // 2, body, 0)


def _natten(qkvg, bias, batch, seq):
    rows = seq // GRID_W
    group = _pick(rows // 2, 8)
    assert seq % GRID_W == 0 and rows >= NA_KH, "token grid must hold a full attention window"
    assert rows % (2 * group) == 0

    def qkv_spec(slot):
        return pl.BlockSpec((None, None, seq, PAIR_W), lambda b, p: (slot, p, b, 0))

    return pl.pallas_call(
        functools.partial(_natten_kernel, rows=rows, group=group),
        grid=(batch, N_PAIR),
        in_specs=[
            qkv_spec(0), qkv_spec(1), qkv_spec(2),
            pl.BlockSpec((None, NA_KH, PAIR_W, NA_KH * GRID_W), lambda b, p: (p, 0, 0, 0)),
        ],
        out_specs=pl.BlockSpec((None, seq, PAIR_W), lambda b, p: (p, b, 0)),
        out_shape=jax.ShapeDtypeStruct((N_PAIR, batch * seq, PAIR_W), F32),
        scratch_shapes=[pltpu.VMEM((2, group, PAIR_W, NA_KH * GRID_W), F32)],
        compiler_params=pltpu.CompilerParams(
            dimension_semantics=("arbitrary", "arbitrary"),
            vmem_limit_bytes=VMEM_LIMIT),
        name="natten",
    )(qkvg, qkvg, qkvg, bias)


def _attn_bias_kernel(rpb_ref, o_ref):
    lane1 = lax.broadcasted_iota(jnp.int32, (1, LANES), 1)
    lane = lax.broadcasted_iota(jnp.int32, (GRID_W, LANES), 1)
    qcol = lax.broadcasted_iota(jnp.int32, (GRID_W, LANES), 0)
    kcol = lane & (GRID_W - 1)
    col_start = jnp.clip(qcol - NA_KW // 2, 0, GRID_W - NA_KW)
    masked = jnp.where((kcol >= col_start) & (kcol < col_start + NA_KW), 0.0, -jnp.inf)
    left = lane < GRID_W
    n_off = NA_KW - 1
    near = (lane1 <= n_off) | (lane1 >= LANES - n_off)
    for hh in range(2):
        r = rpb_ref[hh]
        far = jnp.where(lane1 < GRID_W, r[:, 2 * n_off:2 * n_off + 1], r[:, 0:1])
        g = jnp.where(near, pltpu.roll(r, LANES - n_off, 1), far)
        lo, hi = [], []
        for ri in range(2 * NA_KH - 1):
            gb = jnp.broadcast_to(g[ri:ri + 1, :], (GRID_W, LANES))
            t = pltpu.roll(gb, 0, 1, stride=1, stride_axis=0)
            lo.append(t)
            hi.append(pltpu.roll(t, GRID_W, 1))
        for var in range(NA_KH):
            for j in range(NA_KH // 2):
                ra = 2 * j - var + NA_KH - 1
                o_ref[var, hh * GRID_W:(hh + 1) * GRID_W, j * LANES:(j + 1) * LANES] = (
                    jnp.where(left, lo[ra], hi[ra + 1]) * LOG2E + masked)


def _attn_bias(rpb):
    h, nr, nc = rpb.shape
    rpb = jnp.pad(rpb.astype(F32), ((0, 0), (0, 2 * NA_KH - nr), (0, LANES - nc)))
    return pl.pallas_call(
        _attn_bias_kernel,
        grid=(N_PAIR,),
        in_specs=[pl.BlockSpec((2, 2 * NA_KH, LANES), lambda p: (p, 0, 0))],
        out_specs=pl.BlockSpec((None, NA_KH, PAIR_W, NA_KH * GRID_W), lambda p: (p, 0, 0, 0)),
        out_shape=jax.ShapeDtypeStruct((N_PAIR, NA_KH, PAIR_W, NA_KH * GRID_W), F32),
        compiler_params=pltpu.CompilerParams(dimension_semantics=("arbitrary",)),
        name="attn_bias",
    )(rpb)


def _mix_out_kernel(a_ref, u_ref, g_ref, x_ref, ws_ref, bs_ref, na_ref, nb_ref, wo_ref,
                    o_ref, sg_ref):
    n_pair, tm, _ = a_ref.shape
    da = n_pair * PAIR_W
    lane = lax.broadcasted_iota(jnp.int32, (1, PAIR_W), 1)
    head0 = lane < DH_B
    n_chunk = tm // CHUNK
    for c0 in range(0, n_chunk, 2):
        chunks = range(c0, min(c0 + 2, n_chunk))
        for p in range(H_B // 2):
            csl = slice(p * PAIR_W, (p + 1) * PAIR_W)
            g = jnp.concatenate([g_ref[p, c * CHUNK:(c + 1) * CHUNK, :] for c in chunks], axis=1)
            zero = jnp.zeros_like(g)
            first = jnp.concatenate([head0] * len(chunks), axis=1)
            gg = jnp.concatenate([jnp.where(first, g, zero), jnp.where(first, zero, g)], axis=0)
            mixed = _dot(ws_ref[p], gg)
            for t, c in enumerate(chunks):
                rsl = slice(c * CHUNK, (c + 1) * CHUNK)
                sg_ref[rsl, csl] = u_ref[rsl, csl] * (mixed[:, t * PAIR_W:(t + 1) * PAIR_W] + bs_ref[p])
    a = jnp.concatenate([a_ref[p] for p in range(n_pair)], axis=1)
    an = _rms(a, na_ref[...]).astype(BF16)
    sn = _rms(sg_ref[...], nb_ref[...]).astype(BF16)
    o_ref[...] = x_ref[...] + _dot(an, wo_ref[:da, :]) + _dot(sn, wo_ref[da:, :])


def _mix_out(a, u, qkvg, x, ws, bs, na, nb, wo):
    n, d = x.shape
    n_pair = a.shape[0]
    da = n_pair * PAIR_W
    db = u.shape[1]
    tm = _pick(n, 512)
    return pl.pallas_call(
        _mix_out_kernel,
        grid=(n // tm,),
        in_specs=[
            pl.BlockSpec((n_pair, tm, PAIR_W), lambda i: (0, i, 0)),
            pl.BlockSpec((tm, db), lambda i: (i, 0)),
            pl.BlockSpec((None, db // PAIR_W, tm, PAIR_W), lambda i: (3, 0, i, 0)),
            pl.BlockSpec((tm, d), lambda i: (i, 0)),
            pl.BlockSpec(ws.shape, lambda i: (0, 0, 0)),
            pl.BlockSpec(bs.shape, lambda i: (0, 0, 0)),
            pl.BlockSpec((1, da), lambda i: (0, 0)),
            pl.BlockSpec((1, db), lambda i: (0, 0)),
            pl.BlockSpec(wo.shape, lambda i: (0, 0)),
        ],
        out_specs=pl.BlockSpec((tm, d), lambda i: (i, 0)),
        out_shape=jax.ShapeDtypeStruct((n, d), F32),
        scratch_shapes=[pltpu.VMEM((tm, db), F32)],
        compiler_params=pltpu.CompilerParams(
            dimension_semantics=("arbitrary",),
            vmem_limit_bytes=VMEM_LIMIT),
        name="mix_out",
    )(a, u, qkvg, x, ws, bs, na, nb, wo)


def _layer(x, p):
    b, t, d = x.shape
    x = x.reshape(b * t, d)
    x1 = _ffn(x, p["ffn1_norm"], p["ffn1_w_in"], p["ffn1_w_out"])
    qkvg, u = _mix_in(x1, p["mix_norm"], p["w_in_mix"], p["q_norm"], p["k_norm"],
                      p["gate_norm"])
    a = _natten(qkvg, p["attn_bias"], b, t)
    x2 = _mix_out(a, u, qkvg, x1, p["w_spatial"], p["b_spatial"], p["out_norm_a"],
                  p["out_norm_b"], p["w_out_mix"])
    y = _ffn(x2, p["ffn2_norm"], p["ffn2_w_in"], p["ffn2_w_out"], p["final_norm"])
    return y.reshape(b, t, d)


def _prep(l, ffn1_norm, ffn1_w_in, ffn1_w_out, mix_norm, w_in_mix, q_norm, k_norm, attn_rpb,
          gate_norm, w_spatial, b_spatial, out_norm_a, out_norm_b, w_out_mix, ffn2_norm,
          ffn2_w_in, ffn2_w_out, final_norm):
    row = lambda v: v[l].astype(F32).reshape(1, -1)
    lane = jnp.arange(PAIR_W)
    bs = b_spatial[l].astype(F32).reshape(H_B // 2, 2, CHUNK)[:, lane // DH_B, :].transpose(0, 2, 1)
    return dict(
        ffn1_norm=row(ffn1_norm), ffn1_w_in=ffn1_w_in[l].astype(BF16),
        ffn1_w_out=ffn1_w_out[l].astype(BF16),
        mix_norm=row(mix_norm), w_in_mix=w_in_mix[l].astype(BF16),
        q_norm=jnp.tile(row(q_norm), (1, H_A)), k_norm=jnp.tile(row(k_norm), (1, H_A)),
        attn_bias=_attn_bias(attn_rpb[l]), gate_norm=row(gate_norm),
        w_spatial=w_spatial[l].astype(BF16).reshape(H_B // 2, 2, CHUNK, CHUNK).transpose(
            0, 2, 1, 3).reshape(H_B // 2, CHUNK, 2 * CHUNK),
        b_spatial=bs, out_norm_a=row(out_norm_a), out_norm_b=row(out_norm_b),
        w_out_mix=w_out_mix[l].astype(BF16),
        ffn2_norm=row(ffn2_norm), ffn2_w_in=ffn2_w_in[l].astype(BF16),
        ffn2_w_out=ffn2_w_out[l].astype(BF16), final_norm=row(final_norm),
    )


def kernel(x_prompt, x_sample, ffn1_norm, ffn1_w_in, ffn1_w_out, mix_norm, w_in_mix, q_norm, k_norm, attn_rpb, gate_norm, w_spatial, b_spatial, out_norm_a, out_norm_b, w_out_mix, ffn2_norm, ffn2_w_in, ffn2_w_out, final_norm):
    weights = (ffn1_norm, ffn1_w_in, ffn1_w_out, mix_norm, w_in_mix, q_norm, k_norm, attn_rpb,
               gate_norm, w_spatial, b_spatial, out_norm_a, out_norm_b, w_out_mix, ffn2_norm,
               ffn2_w_in, ffn2_w_out, final_norm)
    depth = ffn1_norm.shape[0]
    layers = [_prep(l, *weights) for l in range(depth)]

    def run(x):
        for p in layers:
            x = _layer(x, p)
        return x

    return (run(x_prompt), run(x_sample))
```

```python
import functools

import jax
import jax.numpy as jnp
from jax import lax
from jax.experimental import pallas as pl
from jax.experimental.pallas import tpu as pltpu

EPS = 1e-6
LOG2E = 1.4426950408889634
GRID_W = 64
NA_KH = 8
NA_KW = 16
H_A = 16
DH_A = 64
H_B = 16
DH_B = 64
CHUNK = 128
LANES = 128
PAIR_W = 2 * DH_A
N_PAIR = H_A // 2
V7X_VMEM_BYTES = 64 * 1024 * 1024
VMEM_LIMIT = V7X_VMEM_BYTES - 6 * 1024 * 1024

F32 = jnp.float32
BF16 = jnp.bfloat16


def _rms(x, g):
    ms = jnp.mean(x * x, axis=-1, keepdims=True)
    return x * lax.rsqrt(ms + EPS) * g


def _gelu(x):
    return 0.5 * x * (1.0 + lax.erf(x * (2.0 ** -0.5)))


def _dot(a, b):
    return jnp.dot(a, b, preferred_element_type=F32)


def _pick(total, pref):
    b = min(total, pref)
    while total % b:
        b //= 2
    return b


def _ffn_kernel(x_ref, nrm_ref, wg_ref, wu_ref, wo_ref, *rest, final_norm, n_chunks):
    if final_norm:
        fn_ref, o_ref, h_ref = rest
    else:
        o_ref, h_ref = rest
    k = pl.program_id(1)

    def step(first, final):
        if first:
            h_ref[...] = _rms(x_ref[...], nrm_ref[...]).astype(BF16)
        h = h_ref[...]
        tf = wg_ref.shape[1]
        n_split = 2 if (tf % (2 * LANES) == 0 and not final) else 1
        for c in range(n_split):
            cs = slice(c * tf // n_split, (c + 1) * tf // n_split)
            g = _dot(h, wg_ref[:, cs])
            u = _dot(h, wu_ref[:, cs])
            a = ((0.5 * g) * jax.nn.sigmoid(g) * u).astype(BF16)
            o_ref[...] = (x_ref[...] if first and c == 0 else o_ref[...]) + _dot(a, wo_ref[cs, :])
        if final:
            o_ref[...] = _rms(o_ref[...], fn_ref[...])

    if n_chunks == 1:
        step(True, final_norm)
    elif final_norm:
        pl.when(k == 0)(lambda: step(True, False))
        pl.when((k > 0) & (k < n_chunks - 1))(lambda: step(False, False))
        pl.when(k == n_chunks - 1)(lambda: step(False, True))
    else:
        pl.when(k == 0)(lambda: step(True, False))
        pl.when(k > 0)(lambda: step(False, False))


def _ffn(x, nrm, w_in, w_out, final_nrm=None):
    n, d = x.shape
    d_ff = w_out.shape[0]
    tm = _pick(n, 1024)
    tf = _pick(d_ff, 512)
    nk = d_ff // tf
    final = final_nrm is not None
    in_specs = [
        pl.BlockSpec((tm, d), lambda i, k: (i, 0)),
        pl.BlockSpec((1, d), lambda i, k: (0, 0)),
        pl.BlockSpec((d, tf), lambda i, k: (0, k)),
        pl.BlockSpec((d, tf), lambda i, k: (0, k + nk)),
        pl.BlockSpec((tf, d), lambda i, k: (k, 0)),
    ]
    args = [x, nrm, w_in, w_in, w_out]
    if final:
        in_specs.append(pl.BlockSpec((1, d), lambda i, k: (0, 0)))
        args.append(final_nrm)
    return pl.pallas_call(
        functools.partial(_ffn_kernel, final_norm=final, n_chunks=nk),
        grid=(n // tm, nk),
        in_specs=in_specs,
        out_specs=pl.BlockSpec((tm, d), lambda i, k: (i, 0)),
        out_shape=jax.ShapeDtypeStruct((n, d), F32),
        scratch_shapes=[pltpu.VMEM((tm, d), BF16)],
        compiler_params=pltpu.CompilerParams(
            dimension_semantics=("arbitrary", "arbitrary"),
            vmem_limit_bytes=VMEM_LIMIT),
        name="ffn_final" if final else "ffn",
    )(*args)


def _head_mean_sq(z):
    lane = lax.broadcasted_iota(jnp.int32, (1, LANES), 1)
    head0 = lane < DH_A
    parts = []
    for c in range(z.shape[1] // LANES):
        blk = z[:, c * LANES:(c + 1) * LANES]
        sq = blk * blk
        a = jnp.sum(jnp.where(head0, sq, 0.0), axis=-1, keepdims=True)
        b = jnp.sum(jnp.where(head0, 0.0, sq), axis=-1, keepdims=True)
        parts.append(jnp.where(head0, a, b) * (1.0 / DH_A))
    return jnp.concatenate(parts, axis=1)


def _mix_in_kernel(x_ref, nrm_ref, w_ref, qn_ref, kn_ref, gn_ref,
                   qkvg_ref, u_ref, h_ref):
    da = u_ref.shape[1]
    h_ref[...] = _rms(x_ref[...], nrm_ref[...]).astype(BF16)

    def z(j):
        return _dot(h_ref[...], w_ref[:, j * da:(j + 1) * da])

    def head_norm(zj, gain):
        return zj * lax.rsqrt(_head_mean_sq(zj) + EPS) * gain

    def put(slot, val):
        val = val.astype(BF16)
        for p in range(da // PAIR_W):
            qkvg_ref[slot, p] = val[:, p * PAIR_W:(p + 1) * PAIR_W]

    put(3, _rms(_gelu(z(4)), gn_ref[...]))
    put(0, head_norm(z(0), qn_ref[...]) * (DH_A ** -0.5 * LOG2E))
    put(1, head_norm(z(1), kn_ref[...]))
    u_ref[...] = _gelu(z(3))
    put(2, z(2))


def _mix_in(x, nrm, w, qn, kn, gn):
    n, d = x.shape
    da = H_A * DH_A
    tm = _pick(n, 512)
    return pl.pallas_call(
        _mix_in_kernel,
        grid=(n // tm,),
        in_specs=[
            pl.BlockSpec((tm, d), lambda i: (i, 0)),
            pl.BlockSpec((1, d), lambda i: (0, 0)),
            pl.BlockSpec(w.shape, lambda i: (0, 0), pipeline_mode=pl.Buffered(1)),
            pl.BlockSpec((1, da), lambda i: (0, 0)),
            pl.BlockSpec((1, da), lambda i: (0, 0)),
            pl.BlockSpec((1, da), lambda i: (0, 0)),
        ],
        out_specs=[
            pl.BlockSpec((4, da // PAIR_W, tm, PAIR_W), lambda i: (0, 0, i, 0)),
            pl.BlockSpec((tm, da), lambda i: (i, 0)),
        ],
        out_shape=[
            jax.ShapeDtypeStruct((4, da // PAIR_W, n, PAIR_W), BF16),
            jax.ShapeDtypeStruct((n, da), F32),
        ],
        scratch_shapes=[pltpu.VMEM((tm, d), BF16)],
        compiler_params=pltpu.CompilerParams(
            dimension_semantics=("arbitrary",),
            vmem_limit_bytes=VMEM_LIMIT),
        name="mix_in",
    )(x, nrm, w, qn, kn, gn)


def _natten_kernel(q_ref, k_ref, v_ref, bias_ref, o_ref, s_ref, *, rows, group):
    nkeys = NA_KH * GRID_W
    n_groups = rows // group
    lane = lax.broadcasted_iota(jnp.int32, (1, PAIR_W), 1)
    head0 = lane < DH_A

    def tokens(row, n):
        return pl.ds(pl.multiple_of(row * GRID_W, GRID_W), n)

    def row_start(r):
        return jnp.clip(r - NA_KH // 2, 0, rows - NA_KH)

    def scores(g, slot):
        for t in range(group):
            r = g * group + t
            rs = row_start(r)
            q = q_ref[tokens(r, GRID_W), :]
            zero = jnp.zeros_like(q)
            qm = jnp.concatenate([jnp.where(head0, q, zero), jnp.where(head0, zero, q)], axis=0)
            s = lax.dot_general(qm, k_ref[tokens(rs, nkeys), :], (((1,), (1,)), ((), ())),
                                preferred_element_type=F32)
            s_ref[slot, t] = s + bias_ref[r - rs]

    def attend(g, slot):
        for t in range(group):
            r = g * group + t
            s = s_ref[slot, t]
            m = jnp.max(s, axis=-1, keepdims=True)
            p = jnp.exp2(s - m)
            l = jnp.sum(p, axis=-1, keepdims=True)
            o = _dot(p.astype(BF16), v_ref[tokens(row_start(r), nkeys), :]) / l
            o_ref[tokens(r, GRID_W), :] = jnp.where(head0, o[:GRID_W], o[GRID_W:])

    scores(0, 0)

    def body(j, carry):
        attend(2 * j, 0)
        scores(2 * j + 1, 1)
        attend(2 * j + 1, 1)
        scores(jnp.minimum(2 * j + 2, n_groups - 1), 0)
        return carry

    lax.fori_loop(0, n_groups // 2, body, 0)


def _natten(qkvg, bias, batch, seq):
    rows = seq // GRID_W
    group = _pick(rows // 2, 8)
    assert seq % GRID_W == 0 and rows >= NA_KH, "token grid must hold a full attention window"
    assert rows % (2 * group) == 0

    def qkv_spec(slot):
        return pl.BlockSpec((None, None, seq, PAIR_W), lambda b, p: (slot, p, b, 0))

    return pl.pallas_call(
        functools.partial(_natten_kernel, rows=rows, group=group),
        grid=(batch, N_PAIR),
        in_specs=[
            qkv_spec(0), qkv_spec(1), qkv_spec(2),
            pl.BlockSpec((None, NA_KH, PAIR_W, NA_KH * GRID_W), lambda b, p: (p, 0, 0, 0)),
        ],
        out_specs=pl.BlockSpec((None, seq, PAIR_W), lambda b, p: (p, b, 0)),
        out_shape=jax.ShapeDtypeStruct((N_PAIR, batch * seq, PAIR_W), F32),
        scratch_shapes=[pltpu.VMEM((2, group, PAIR_W, NA_KH * GRID_W), F32)],
        compiler_params=pltpu.CompilerParams(
            dimension_semantics=("arbitrary", "arbitrary"),
            vmem_limit_bytes=VMEM_LIMIT),
        name="natten",
    )(qkvg, qkvg, qkvg, bias)


def _attn_bias_kernel(rpb_ref, o_ref):
    lane1 = lax.broadcasted_iota(jnp.int32, (1, LANES), 1)
    lane = lax.broadcasted_iota(jnp.int32, (GRID_W, LANES), 1)
    qcol = lax.broadcasted_iota(jnp.int32, (GRID_W, LANES), 0)
    kcol = lane & (GRID_W - 1)
    col_start = jnp.clip(qcol - NA_KW // 2, 0, GRID_W - NA_KW)
    masked = jnp.where((kcol >= col_start) & (kcol < col_start + NA_KW), 0.0, -jnp.inf)
    left = lane < GRID_W
    n_off = NA_KW - 1
    near = (lane1 <= n_off) | (lane1 >= LANES - n_off)
    for hh in range(2):
        r = rpb_ref[hh]
        far = jnp.where(lane1 < GRID_W, r[:, 2 * n_off:2 * n_off + 1], r[:, 0:1])
        g = jnp.where(near, pltpu.roll(r, LANES - n_off, 1), far)
        lo, hi = [], []
        for ri in range(2 * NA_KH - 1):
            gb = jnp.broadcast_to(g[ri:ri + 1, :], (GRID_W, LANES))
            t = pltpu.roll(gb, 0, 1, stride=1, stride_axis=0)
            lo.append(t)
            hi.append(pltpu.roll(t, GRID_W, 1))
        for var in range(NA_KH):
            for j in range(NA_KH // 2):
                ra = 2 * j - var + NA_KH - 1
                o_ref[var, hh * GRID_W:(hh + 1) * GRID_W, j * LANES:(j + 1) * LANES] = (
                    jnp.where(left, lo[ra], hi[ra + 1]) * LOG2E + masked)


def _attn_bias(rpb):
    h, nr, nc = rpb.shape
    rpb = jnp.pad(rpb.astype(F32), ((0, 0), (0, 2 * NA_KH - nr), (0, LANES - nc)))
    return pl.pallas_call(
        _attn_bias_kernel,
        grid=(N_PAIR,),
        in_specs=[pl.BlockSpec((2, 2 * NA_KH, LANES), lambda p: (p, 0, 0))],
        out_specs=pl.BlockSpec((None, NA_KH, PAIR_W, NA_KH * GRID_W), lambda p: (p, 0, 0, 0)),
        out_shape=jax.ShapeDtypeStruct((N_PAIR, NA_KH, PAIR_W, NA_KH * GRID_W), F32),
        compiler_params=pltpu.CompilerParams(dimension_semantics=("arbitrary",)),
        name="attn_bias",
    )(rpb)


def _mix_out_kernel(a_ref, u_ref, g_ref, x_ref, ws_ref, bs_ref, na_ref, nb_ref, wo_ref,
                    o_ref, sg_ref):
    n_pair, tm, _ = a_ref.shape
    da = n_pair * PAIR_W
    lane = lax.broadcasted_iota(jnp.int32, (1, PAIR_W), 1)
    head0 = lane < DH_B
    n_chunk = tm // CHUNK
    for c0 in range(0, n_chunk, 2):
        chunks = range(c0, min(c0 + 2, n_chunk))
        for p in range(H_B // 2):
            csl = slice(p * PAIR_W, (p + 1) * PAIR_W)
            g = jnp.concatenate([g_ref[p, c * CHUNK:(c + 1) * CHUNK, :] for c in chunks], axis=1)
            zero = jnp.zeros_like(g)
            first = jnp.concatenate([head0] * len(chunks), axis=1)
            gg = jnp.concatenate([jnp.where(first, g, zero), jnp.where(first, zero, g)], axis=0)
            mixed = _dot(ws_ref[p], gg)
            for t, c in enumerate(chunks):
                rsl = slice(c * CHUNK, (c + 1) * CHUNK)
                sg_ref[rsl, csl] = u_ref[rsl, csl] * (mixed[:, t * PAIR_W:(t + 1) * PAIR_W] + bs_ref[p])
    a = jnp.concatenate([a_ref[p] for p in range(n_pair)], axis=1)
    an = _rms(a, na_ref[...]).astype(BF16)
    sn = _rms(sg_ref[...], nb_ref[...]).astype(BF16)
    o_ref[...] = x_ref[...] + _dot(an, wo_ref[:da, :]) + _dot(sn, wo_ref[da:, :])


def _mix_out(a, u, qkvg, x, ws, bs, na, nb, wo):
    n, d = x.shape
    n_pair = a.shape[0]
    da = n_pair * PAIR_W
    db = u.shape[1]
    tm = _pick(n, 512)
    return pl.pallas_call(
        _mix_out_kernel,
        grid=(n // tm,),
        in_specs=[
            pl.BlockSpec((n_pair, tm, PAIR_W), lambda i: (0, i, 0)),
            pl.BlockSpec((tm, db), lambda i: (i, 0)),
            pl.BlockSpec((None, db // PAIR_W, tm, PAIR_W), lambda i: (3, 0, i, 0)),
            pl.BlockSpec((tm, d), lambda i: (i, 0)),
            pl.BlockSpec(ws.shape, lambda i: (0, 0, 0)),
            pl.BlockSpec(bs.shape, lambda i: (0, 0, 0)),
            pl.BlockSpec((1, da), lambda i: (0, 0)),
            pl.BlockSpec((1, db), lambda i: (0, 0)),
            pl.BlockSpec(wo.shape, lambda i: (0, 0)),
        ],
        out_specs=pl.BlockSpec((tm, d), lambda i: (i, 0)),
        out_shape=jax.ShapeDtypeStruct((n, d), F32),
        scratch_shapes=[pltpu.VMEM((tm, db), F32)],
        compiler_params=pltpu.CompilerParams(
            dimension_semantics=("arbitrary",),
            vmem_limit_bytes=VMEM_LIMIT),
        name="mix_out",
    )(a, u, qkvg, x, ws, bs, na, nb, wo)


def _layer(x, p):
    b, t, d = x.shape
    x = x.reshape(b * t, d)
    x1 = _ffn(x, p["ffn1_norm"], p["ffn1_w_in"], p["ffn1_w_out"])
    qkvg, u = _mix_in(x1, p["mix_norm"], p["w_in_mix"], p["q_norm"], p["k_norm"],
                      p["gate_norm"])
    a = _natten(qkvg, p["attn_bias"], b, t)
    x2 = _mix_out(a, u, qkvg, x1, p["w_spatial"], p["b_spatial"], p["out_norm_a"],
                  p["out_norm_b"], p["w_out_mix"])
    y = _ffn(x2, p["ffn2_norm"], p["ffn2_w_in"], p["ffn2_w_out"], p["final_norm"])
    return y.reshape(b, t, d)


def _prep(l, ffn1_norm, ffn1_w_in, ffn1_w_out, mix_norm, w_in_mix, q_norm, k_norm, attn_rpb,
          gate_norm, w_spatial, b_spatial, out_norm_a, out_norm_b, w_out_mix, ffn2_norm,
          ffn2_w_in, ffn2_w_out, final_norm):
    row = lambda v: v[l].astype(F32).reshape(1, -1)
    lane = jnp.arange(PAIR_W)
    bs = b_spatial[l].astype(F32).reshape(H_B // 2, 2, CHUNK)[:, lane // DH_B, :].transpose(0, 2, 1)
    return dict(
        ffn1_norm=row(ffn1_norm), ffn1_w_in=ffn1_w_in[l].astype(BF16),
        ffn1_w_out=ffn1_w_out[l].astype(BF16),
        mix_norm=row(mix_norm), w_in_mix=w_in_mix[l].astype(BF16),
        q_norm=jnp.tile(row(q_norm), (1, H_A)), k_norm=jnp.tile(row(k_norm), (1, H_A)),
        attn_bias=_attn_bias(attn_rpb[l]), gate_norm=row(gate_norm),
        w_spatial=w_spatial[l].astype(BF16).reshape(H_B // 2, 2, CHUNK, CHUNK).transpose(
            0, 2, 1, 3).reshape(H_B // 2, CHUNK, 2 * CHUNK),
        b_spatial=bs, out_norm_a=row(out_norm_a), out_norm_b=row(out_norm_b),
        w_out_mix=w_out_mix[l].astype(BF16),
        ffn2_norm=row(ffn2_norm), ffn2_w_in=ffn2_w_in[l].astype(BF16),
        ffn2_w_out=ffn2_w_out[l].astype(BF16), final_norm=row(final_norm),
    )


def kernel(x_prompt, x_sample, ffn1_norm, ffn1_w_in, ffn1_w_out, mix_norm, w_in_mix, q_norm, k_norm, attn_rpb, gate_norm, w_spatial, b_spatial, out_norm_a, out_norm_b, w_out_mix, ffn2_norm, ffn2_w_in, ffn2_w_out, final_norm):
    weights = (ffn1_norm, ffn1_w_in, ffn1_w_out, mix_norm, w_in_mix, q_norm, k_norm, attn_rpb,
               gate_norm, w_spatial, b_spatial, out_norm_a, out_norm_b, w_out_mix, ffn2_norm,
               ffn2_w_in, ffn2_w_out, final_norm)
    depth = ffn1_norm.shape[0]
    layers = [_prep(l, *weights) for l in range(depth)]

    def run(x):
        for p in layers:
            x = _layer(x, p)
        return x

    return (run(x_prompt), run(x_sample))
```

```python
import functools

import jax
import jax.numpy as jnp
from jax import lax
from jax.experimental import pallas as pl
from jax.experimental.pallas import tpu as pltpu

EPS = 1e-6
LOG2E = 1.4426950408889634
GRID_W = 64
NA_KH = 8
NA_KW = 16
H_A = 16
DH_A = 64
H_B = 16
DH_B = 64
CHUNK = 128
LANES = 128
PAIR_W = 2 * DH_A
N_PAIR = H_A // 2
V7X_VMEM_BYTES = 64 * 1024 * 1024
VMEM_LIMIT = V7X_VMEM_BYTES - 6 * 1024 * 1024

F32 = jnp.float32
BF16 = jnp.bfloat16


def _rms(x, g):
    ms = jnp.mean(x * x, axis=-1, keepdims=True)
    return x * lax.rsqrt(ms + EPS) * g


def _gelu(x):
    return 0.5 * x * (1.0 + lax.erf(x * (2.0 ** -0.5)))


def _dot(a, b):
    return jnp.dot(a, b, preferred_element_type=F32)


def _pick(total, pref):
    b = min(total, pref)
    while total % b:
        b //= 2
    return b


def _ffn_kernel(x_ref, nrm_ref, wg_ref, wu_ref, wo_ref, *rest, final_norm, n_chunks):
    if final_norm:
        fn_ref, o_ref, h_ref = rest
    else:
        o_ref, h_ref = rest
    k = pl.program_id(1)

    def step(first, final):
        if first:
            h_ref[...] = _rms(x_ref[...], nrm_ref[...]).astype(BF16)
        h = h_ref[...]
        tf = wg_ref.shape[1]
        n_split = 2 if (tf % (2 * LANES) == 0 and not final) else 1
        for c in range(n_split):
            cs = slice(c * tf // n_split, (c + 1) * tf // n_split)
            g = _dot(h, wg_ref[:, cs])
            u = _dot(h, wu_ref[:, cs])
            a = ((0.5 * g) * jax.nn.sigmoid(g) * u).astype(BF16)
            wo = wo_ref[cs, :].astype(BF16)
            o_ref[...] = (x_ref[...] if first and c == 0 else o_ref[...]) + _dot(a, wo)
        if final:
            o_ref[...] = _rms(o_ref[...], fn_ref[...])

    if n_chunks == 1:
        step(True, final_norm)
    elif final_norm:
        pl.when(k == 0)(lambda: step(True, False))
        pl.when((k > 0) & (k < n_chunks - 1))(lambda: step(False, False))
        pl.when(k == n_chunks - 1)(lambda: step(False, True))
    else:
        pl.when(k == 0)(lambda: step(True, False))
        pl.when(k > 0)(lambda: step(False, False))


def _ffn(x, nrm, w_in, w_out, final_nrm=None):
    n, d = x.shape
    d_ff = w_out.shape[0]
    tm = _pick(n, 1024)
    tf = _pick(d_ff, 512)
    nk = d_ff // tf
    final = final_nrm is not None
    in_specs = [
        pl.BlockSpec((tm, d), lambda i, k: (i, 0)),
        pl.BlockSpec((1, d), lambda i, k: (0, 0)),
        pl.BlockSpec((d, tf), lambda i, k: (0, k)),
        pl.BlockSpec((d, tf), lambda i, k: (0, k + nk)),
        pl.BlockSpec((tf, d), lambda i, k: (k, 0)),
    ]
    args = [x, nrm, w_in, w_in, w_out]
    if final:
        in_specs.append(pl.BlockSpec((1, d), lambda i, k: (0, 0)))
        args.append(final_nrm)
    return pl.pallas_call(
        functools.partial(_ffn_kernel, final_norm=final, n_chunks=nk),
        grid=(n // tm, nk),
        in_specs=in_specs,
        out_specs=pl.BlockSpec((tm, d), lambda i, k: (i, 0)),
        out_shape=jax.ShapeDtypeStruct((n, d), F32),
        scratch_shapes=[pltpu.VMEM((tm, d), BF16)],
        compiler_params=pltpu.CompilerParams(
            dimension_semantics=("arbitrary", "arbitrary"),
            vmem_limit_bytes=VMEM_LIMIT),
        name="ffn_final" if final else "ffn",
    )(*args)


def _head_mean_sq(z):
    lane = lax.broadcasted_iota(jnp.int32, (1, LANES), 1)
    head0 = lane < DH_A
    parts = []
    for c in range(z.shape[1] // LANES):
        blk = z[:, c * LANES:(c + 1) * LANES]
        sq = blk * blk
        a = jnp.sum(jnp.where(head0, sq, 0.0), axis=-1, keepdims=True)
        b = jnp.sum(jnp.where(head0, 0.0, sq), axis=-1, keepdims=True)
        parts.append(jnp.where(head0, a, b) * (1.0 / DH_A))
    return jnp.concatenate(parts, axis=1)


def _mix_in_kernel(x_ref, nrm_ref, w_ref, qn_ref, kn_ref, gn_ref,
                   qkvg_ref, u_ref, h_ref):
    da = u_ref.shape[1]
    h_ref[...] = _rms(x_ref[...], nrm_ref[...]).astype(BF16)

    def z(j):
        return _dot(h_ref[...], w_ref[:, j * da:(j + 1) * da])

    def head_norm(zj, gain):
        return zj * lax.rsqrt(_head_mean_sq(zj) + EPS) * gain

    def put(slot, val):
        val = val.astype(BF16)
        for p in range(da // PAIR_W):
            qkvg_ref[slot, p] = val[:, p * PAIR_W:(p + 1) * PAIR_W]

    put(3, _rms(_gelu(z(4)), gn_ref[...]))
    put(0, head_norm(z(0), qn_ref[...]) * (DH_A ** -0.5 * LOG2E))
    put(1, head_norm(z(1), kn_ref[...]))
    u_ref[...] = _gelu(z(3))
    put(2, z(2))


def _mix_in(x, nrm, w, qn, kn, gn):
    n, d = x.shape
    da = H_A * DH_A
    tm = _pick(n, 512)
    return pl.pallas_call(
        _mix_in_kernel,
        grid=(n // tm,),
        in_specs=[
            pl.BlockSpec((tm, d), lambda i: (i, 0)),
            pl.BlockSpec((1, d), lambda i: (0, 0)),
            pl.BlockSpec(w.shape, lambda i: (0, 0), pipeline_mode=pl.Buffered(1)),
            pl.BlockSpec((1, da), lambda i: (0, 0)),
            pl.BlockSpec((1, da), lambda i: (0, 0)),
            pl.BlockSpec((1, da), lambda i: (0, 0)),
        ],
        out_specs=[
            pl.BlockSpec((4, da // PAIR_W, tm, PAIR_W), lambda i: (0, 0, i, 0)),
            pl.BlockSpec((tm, da), lambda i: (i, 0)),
        ],
        out_shape=[
            jax.ShapeDtypeStruct((4, da // PAIR_W, n, PAIR_W), BF16),
            jax.ShapeDtypeStruct((n, da), F32),
        ],
        scratch_shapes=[pltpu.VMEM((tm, d), BF16)],
        compiler_params=pltpu.CompilerParams(
            dimension_semantics=("arbitrary",),
            vmem_limit_bytes=VMEM_LIMIT),
        name="mix_in",
    )(x, nrm, w, qn, kn, gn)


def _natten_kernel(q_ref, k_ref, v_ref, bias_ref, o_ref, s_ref, *, rows, group):
    nkeys = NA_KH * GRID_W
    n_groups = rows // group
    lane = lax.broadcasted_iota(jnp.int32, (1, PAIR_W), 1)
    head0 = lane < DH_A

    def tokens(row, n):
        return pl.ds(pl.multiple_of(row * GRID_W, GRID_W), n)

    def row_start(r):
        return jnp.clip(r - NA_KH // 2, 0, rows - NA_KH)

    def scores(g, slot):
        for t in range(group):
            r = g * group + t
            rs = row_start(r)
            q = q_ref[tokens(r, GRID_W), :]
            zero = jnp.zeros_like(q)
            qm = jnp.concatenate([jnp.where(head0, q, zero), jnp.where(head0, zero, q)], axis=0)
            s = lax.dot_general(qm, k_ref[tokens(rs, nkeys), :], (((1,), (1,)), ((), ())),
                                preferred_element_type=F32)
            s_ref[slot, t] = s + bias_ref[r - rs]

    def attend(g, slot):
        for t in range(group):
            r = g * group + t
            s = s_ref[slot, t]
            m = jnp.max(s, axis=-1, keepdims=True)
            p = jnp.exp2(s - m)
            l = jnp.sum(p, axis=-1, keepdims=True)
            o = _dot(p.astype(BF16), v_ref[tokens(row_start(r), nkeys), :]) / l
            o_ref[tokens(r, GRID_W), :] = jnp.where(head0, o[:GRID_W], o[GRID_W:])

    scores(0, 0)

    def body(j, carry):
        attend(2 * j, 0)
        scores(2 * j + 1, 1)
        attend(2 * j + 1, 1)
        scores(jnp.minimum(2 * j + 2, n_groups - 1), 0)
        return carry

    lax.fori_loop(0, n_groups // 2, body, 0)


def _natten(qkvg, bias, batch, seq):
    rows = seq // GRID_W
    group = _pick(rows // 2, 8)
    assert seq % GRID_W == 0 and rows >= NA_KH, "token grid must hold a full attention window"
    assert rows % (2 * group) == 0

    def qkv_spec(slot):
        return pl.BlockSpec((None, None, seq, PAIR_W), lambda b, p: (slot, p, b, 0))

    return pl.pallas_call(
        functools.partial(_natten_kernel, rows=rows, group=group),
        grid=(batch, N_PAIR),
        in_specs=[
            qkv_spec(0), qkv_spec(1), qkv_spec(2),
            pl.BlockSpec((None, NA_KH, PAIR_W, NA_KH * GRID_W), lambda b, p: (p, 0, 0, 0)),
        ],
        out_specs=pl.BlockSpec((None, seq, PAIR_W), lambda b, p: (p, b, 0)),
        out_shape=jax.ShapeDtypeStruct((N_PAIR, batch * seq, PAIR_W), F32),
        scratch_shapes=[pltpu.VMEM((2, group, PAIR_W, NA_KH * GRID_W), F32)],
        compiler_params=pltpu.CompilerParams(
            dimension_semantics=("arbitrary", "arbitrary"),
            vmem_limit_bytes=VMEM_LIMIT),
        name="natten",
    )(qkvg, qkvg, qkvg, bias)


def _attn_bias_kernel(rpb_ref, o_ref):
    lane1 = lax.broadcasted_iota(jnp.int32, (1, LANES), 1)
    lane = lax.broadcasted_iota(jnp.int32, (GRID_W, LANES), 1)
    qcol = lax.broadcasted_iota(jnp.int32, (GRID_W, LANES), 0)
    kcol = lane & (GRID_W - 1)
    col_start = jnp.clip(qcol - NA_KW // 2, 0, GRID_W - NA_KW)
    masked = jnp.where((kcol >= col_start) & (kcol < col_start + NA_KW), 0.0, -jnp.inf)
    left = lane < GRID_W
    n_off = NA_KW - 1
    near = (lane1 <= n_off) | (lane1 >= LANES - n_off)
    for hh in range(2):
        r = rpb_ref[hh]
        far = jnp.where(lane1 < GRID_W, r[:, 2 * n_off:2 * n_off + 1], r[:, 0:1])
        g = jnp.where(near, pltpu.roll(r, LANES - n_off, 1), far)
        lo, hi = [], []
        for ri in range(2 * NA_KH - 1):
            gb = jnp.broadcast_to(g[ri:ri + 1, :], (GRID_W, LANES))
            t = pltpu.roll(gb, 0, 1, stride=1, stride_axis=0)
            lo.append(t)
            hi.append(pltpu.roll(t, GRID_W, 1))
        for var in range(NA_KH):
            for j in range(NA_KH // 2):
                ra = 2 * j - var + NA_KH - 1
                o_ref[var, hh * GRID_W:(hh + 1) * GRID_W, j * LANES:(j + 1) * LANES] = (
                    jnp.where(left, lo[ra], hi[ra + 1]) * LOG2E + masked)


def _attn_bias(rpb):
    h, nr, nc = rpb.shape
    rpb = jnp.pad(rpb.astype(F32), ((0, 0), (0, 2 * NA_KH - nr), (0, LANES - nc)))
    return pl.pallas_call(
        _attn_bias_kernel,
        grid=(N_PAIR,),
        in_specs=[pl.BlockSpec((2, 2 * NA_KH, LANES), lambda p: (p, 0, 0))],
        out_specs=pl.BlockSpec((None, NA_KH, PAIR_W, NA_KH * GRID_W), lambda p: (p, 0, 0, 0)),
        out_shape=jax.ShapeDtypeStruct((N_PAIR, NA_KH, PAIR_W, NA_KH * GRID_W), F32),
        compiler_params=pltpu.CompilerParams(dimension_semantics=("arbitrary",)),
        name="attn_bias",
    )(rpb)


def _mix_out_kernel(a_ref, u_ref, g_ref, x_ref, ws_ref, bs_ref, na_ref, nb_ref, wo_ref,
                    o_ref, sg_ref):
    n_pair, tm, _ = a_ref.shape
    da = n_pair * PAIR_W
    lane = lax.broadcasted_iota(jnp.int32, (1, PAIR_W), 1)
    head0 = lane < DH_B
    n_chunk = tm // CHUNK
    for c0 in range(0, n_chunk, 2):
        chunks = range(c0, min(c0 + 2, n_chunk))
        for p in range(H_B // 2):
            csl = slice(p * PAIR_W, (p + 1) * PAIR_W)
            g = jnp.concatenate([g_ref[p, c * CHUNK:(c + 1) * CHUNK, :] for c in chunks], axis=1)
            zero = jnp.zeros_like(g)
            first = jnp.concatenate([head0] * len(chunks), axis=1)
            gg = jnp.concatenate([jnp.where(first, g, zero), jnp.where(first, zero, g)], axis=0)
            mixed = _dot(ws_ref[p], gg)
            for t, c in enumerate(chunks):
                rsl = slice(c * CHUNK, (c + 1) * CHUNK)
                sg_ref[rsl, csl] = u_ref[rsl, csl] * (mixed[:, t * PAIR_W:(t + 1) * PAIR_W] + bs_ref[p])
    a = jnp.concatenate([a_ref[p] for p in range(n_pair)], axis=1)
    an = _rms(a, na_ref[...]).astype(BF16)
    sn = _rms(sg_ref[...], nb_ref[...]).astype(BF16)
    o_ref[...] = x_ref[...] + _dot(an, wo_ref[:da, :]) + _dot(sn, wo_ref[da:, :])


def _mix_out(a, u, qkvg, x, ws, bs, na, nb, wo):
    n, d = x.shape
    n_pair = a.shape[0]
    da = n_pair * PAIR_W
    db = u.shape[1]
    tm = _pick(n, 512)
    return pl.pallas_call(
        _mix_out_kernel,
        grid=(n // tm,),
        in_specs=[
            pl.BlockSpec((n_pair, tm, PAIR_W), lambda i: (0, i, 0)),
            pl.BlockSpec((tm, db), lambda i: (i, 0)),
            pl.BlockSpec((None, db // PAIR_W, tm, PAIR_W), lambda i: (3, 0, i, 0)),
            pl.BlockSpec((tm, d), lambda i: (i, 0)),
            pl.BlockSpec(ws.shape, lambda i: (0, 0, 0)),
            pl.BlockSpec(bs.shape, lambda i: (0, 0, 0)),
            pl.BlockSpec((1, da), lambda i: (0, 0)),
            pl.BlockSpec((1, db), lambda i: (0, 0)),
            pl.BlockSpec(wo.shape, lambda i: (0, 0)),
        ],
        out_specs=pl.BlockSpec((tm, d), lambda i: (i, 0)),
        out_shape=jax.ShapeDtypeStruct((n, d), F32),
        scratch_shapes=[pltpu.VMEM((tm, db), F32)],
        compiler_params=pltpu.CompilerParams(
            dimension_semantics=("arbitrary",),
            vmem_limit_bytes=VMEM_LIMIT),
        name="mix_out",
    )(a, u, qkvg, x, ws, bs, na, nb, wo)


def _layer(x, p):
    b, t, d = x.shape
    x = x.reshape(b * t, d)
    x1 = _ffn(x, p["ffn1_norm"], p["ffn1_w_in"], p["ffn1_w_out"])
    qkvg, u = _mix_in(x1, p["mix_norm"], p["w_in_mix"], p["q_norm"], p["k_norm"],
                      p["gate_norm"])
    a = _natten(qkvg, p["attn_bias"], b, t)
    x2 = _mix_out(a, u, qkvg, x1, p["w_spatial"], p["b_spatial"], p["out_norm_a"],
                  p["out_norm_b"], p["w_out_mix"])
    y = _ffn(x2, p["ffn2_norm"], p["ffn2_w_in"], p["ffn2_w_out"], p["final_norm"])
    return y.reshape(b, t, d)


def _prep(l, ffn1_norm, ffn1_w_in, ffn1_w_out, mix_norm, w_in_mix, q_norm, k_norm, attn_rpb,
          gate_norm, w_spatial, b_spatial, out_norm_a, out_norm_b, w_out_mix, ffn2_norm,
          ffn2_w_in, ffn2_w_out, final_norm):
    row = lambda v: v[l].astype(F32).reshape(1, -1)
    lane = jnp.arange(PAIR_W)
    bs = b_spatial[l].astype(F32).reshape(H_B // 2, 2, CHUNK)[:, lane // DH_B, :].transpose(0, 2, 1)
    return dict(
        ffn1_norm=row(ffn1_norm), ffn1_w_in=ffn1_w_in[l].astype(BF16),
        ffn1_w_out=ffn1_w_out[l],
        mix_norm=row(mix_norm), w_in_mix=w_in_mix[l].astype(BF16),
        q_norm=jnp.tile(row(q_norm), (1, H_A)), k_norm=jnp.tile(row(k_norm), (1, H_A)),
        attn_bias=_attn_bias(attn_rpb[l]), gate_norm=row(gate_norm),
        w_spatial=w_spatial[l].astype(BF16).reshape(H_B // 2, 2, CHUNK, CHUNK).transpose(
            0, 2, 1, 3).reshape(H_B // 2, CHUNK, 2 * CHUNK),
        b_spatial=bs, out_norm_a=row(out_norm_a), out_norm_b=row(out_norm_b),
        w_out_mix=w_out_mix[l].astype(BF16),
        ffn2_norm=row(ffn2_norm), ffn2_w_in=ffn2_w_in[l].astype(BF16),
        ffn2_w_out=ffn2_w_out[l], final_norm=row(final_norm),
    )


def kernel(x_prompt, x_sample, ffn1_norm, ffn1_w_in, ffn1_w_out, mix_norm, w_in_mix, q_norm, k_norm, attn_rpb, gate_norm, w_spatial, b_spatial, out_norm_a, out_norm_b, w_out_mix, ffn2_norm, ffn2_w_in, ffn2_w_out, final_norm):
    weights = (ffn1_norm, ffn1_w_in, ffn1_w_out, mix_norm, w_in_mix, q_norm, k_norm, attn_rpb,
               gate_norm, w_spatial, b_spatial, out_norm_a, out_norm_b, w_out_mix, ffn2_norm,
               ffn2_w_in, ffn2_w_out, final_norm)
    depth = ffn1_norm.shape[0]
    layers = [_prep(l, *weights) for l in range(depth)]

    def run(x):
        for p in layers:
            x = _layer(x, p)
        return x

    return (run(x_prompt), run(x_sample))
```

```python
import functools

import jax
import jax.numpy as jnp
from jax import lax
from jax.experimental import pallas as pl
from jax.experimental.pallas import tpu as pltpu

EPS = 1e-6
LOG2E = 1.4426950408889634
GRID_W = 64
NA_KH = 8
NA_KW = 16
H_A = 16
DH_A = 64
H_B = 16
DH_B = 64
CHUNK = 128
LANES = 128
PAIR_W = 2 * DH_A
N_PAIR = H_A // 2
V7X_VMEM_BYTES = 64 * 1024 * 1024
VMEM_LIMIT = V7X_VMEM_BYTES - 6 * 1024 * 1024

F32 = jnp.float32
BF16 = jnp.bfloat16


def _rms(x, g):
    ms = jnp.mean(x * x, axis=-1, keepdims=True)
    return x * lax.rsqrt(ms + EPS) * g


def _gelu(x):
    return 0.5 * x * (1.0 + lax.erf(x * (2.0 ** -0.5)))


def _dot(a, b):
    return jnp.dot(a, b, preferred_element_type=F32)


def _pick(total, pref):
    b = min(total, pref)
    while total % b:
        b //= 2
    return b


def _ffn_kernel(x_ref, nrm_ref, wg_ref, wu_ref, wo_ref, *rest, final_norm, n_chunks):
    if final_norm:
        fn_ref, o_ref, h_ref = rest
    else:
        o_ref, h_ref = rest
    k = pl.program_id(1)

    def step(first, final):
        if first:
            h_ref[...] = _rms(x_ref[...], nrm_ref[...]).astype(BF16)
        h = h_ref[...]
        tf = wg_ref.shape[1]
        n_split = 2 if (tf % (2 * LANES) == 0 and not final) else 1
        for c in range(n_split):
            cs = slice(c * tf // n_split, (c + 1) * tf // n_split)
            g = _dot(h, wg_ref[:, cs])
            u = _dot(h, wu_ref[:, cs])
            a = ((0.5 * g) * jax.nn.sigmoid(g) * u).astype(BF16)
            wo = wo_ref[cs, :].astype(BF16)
            o_ref[...] = (x_ref[...] if first and c == 0 else o_ref[...]) + _dot(a, wo)
        if final:
            o_ref[...] = _rms(o_ref[...], fn_ref[...])

    if n_chunks == 1:
        step(True, final_norm)
    elif final_norm:
        pl.when(k == 0)(lambda: step(True, False))
        pl.when((k > 0) & (k < n_chunks - 1))(lambda: step(False, False))
        pl.when(k == n_chunks - 1)(lambda: step(False, True))
    else:
        pl.when(k == 0)(lambda: step(True, False))
        pl.when(k > 0)(lambda: step(False, False))


def _ffn(x, nrm, w_in, w_out, final_nrm=None):
    n, d = x.shape
    d_ff = w_out.shape[0]
    tm = _pick(n, 1024)
    tf = _pick(d_ff, 512)
    nk = d_ff // tf
    final = final_nrm is not None
    in_specs = [
        pl.BlockSpec((tm, d), lambda i, k: (i, 0)),
        pl.BlockSpec((1, d), lambda i, k: (0, 0)),
        pl.BlockSpec((d, tf), lambda i, k: (0, k)),
        pl.BlockSpec((d, tf), lambda i, k: (0, k + nk)),
        pl.BlockSpec((tf, d), lambda i, k: (k, 0)),
    ]
    args = [x, nrm, w_in, w_in, w_out]
    if final:
        in_specs.append(pl.BlockSpec((1, d), lambda i, k: (0, 0)))
        args.append(final_nrm)
    return pl.pallas_call(
        functools.partial(_ffn_kernel, final_norm=final, n_chunks=nk),
        grid=(n // tm, nk),
        in_specs=in_specs,
        out_specs=pl.BlockSpec((tm, d), lambda i, k: (i, 0)),
        out_shape=jax.ShapeDtypeStruct((n, d), F32),
        scratch_shapes=[pltpu.VMEM((tm, d), BF16)],
        compiler_params=pltpu.CompilerParams(
            dimension_semantics=("arbitrary", "arbitrary"),
            vmem_limit_bytes=VMEM_LIMIT),
        name="ffn_final" if final else "ffn",
    )(*args)


def _head_mean_sq(z):
    lane = lax.broadcasted_iota(jnp.int32, (1, LANES), 1)
    head0 = lane < DH_A
    parts = []
    for c in range(z.shape[1] // LANES):
        blk = z[:, c * LANES:(c + 1) * LANES]
        sq = blk * blk
        a = jnp.sum(jnp.where(head0, sq, 0.0), axis=-1, keepdims=True)
        b = jnp.sum(jnp.where(head0, 0.0, sq), axis=-1, keepdims=True)
        parts.append(jnp.where(head0, a, b) * (1.0 / DH_A))
    return jnp.concatenate(parts, axis=1)


def _mix_in_kernel(x_ref, nrm_ref, w_ref, qn_ref, kn_ref, gn_ref,
                   qkvg_ref, u_ref, h_ref):
    da = u_ref.shape[1]
    h_ref[...] = _rms(x_ref[...], nrm_ref[...]).astype(BF16)

    def z(j):
        return _dot(h_ref[...], w_ref[:, j * da:(j + 1) * da])

    def head_norm(zj, gain):
        return zj * lax.rsqrt(_head_mean_sq(zj) + EPS) * gain

    def put(slot, val):
        val = val.astype(BF16)
        for p in range(da // PAIR_W):
            qkvg_ref[slot, p] = val[:, p * PAIR_W:(p + 1) * PAIR_W]

    put(3, _rms(_gelu(z(4)), gn_ref[...]))
    put(0, head_norm(z(0), qn_ref[...]) * (DH_A ** -0.5 * LOG2E))
    put(1, head_norm(z(1), kn_ref[...]))
    u_ref[...] = _gelu(z(3))
    put(2, z(2))


def _mix_in(x, nrm, w, qn, kn, gn):
    n, d = x.shape
    da = H_A * DH_A
    tm = _pick(n, 512)
    return pl.pallas_call(
        _mix_in_kernel,
        grid=(n // tm,),
        in_specs=[
            pl.BlockSpec((tm, d), lambda i: (i, 0)),
            pl.BlockSpec((1, d), lambda i: (0, 0)),
            pl.BlockSpec(w.shape, lambda i: (0, 0), pipeline_mode=pl.Buffered(1)),
            pl.BlockSpec((1, da), lambda i: (0, 0)),
            pl.BlockSpec((1, da), lambda i: (0, 0)),
            pl.BlockSpec((1, da), lambda i: (0, 0)),
        ],
        out_specs=[
            pl.BlockSpec((4, da // PAIR_W, tm, PAIR_W), lambda i: (0, 0, i, 0)),
            pl.BlockSpec((tm, da), lambda i: (i, 0)),
        ],
        out_shape=[
            jax.ShapeDtypeStruct((4, da // PAIR_W, n, PAIR_W), BF16),
            jax.ShapeDtypeStruct((n, da), F32),
        ],
        scratch_shapes=[pltpu.VMEM((tm, d), BF16)],
        compiler_params=pltpu.CompilerParams(
            dimension_semantics=("arbitrary",),
            vmem_limit_bytes=VMEM_LIMIT),
        name="mix_in",
    )(x, nrm, w, qn, kn, gn)


def _natten_kernel(q_ref, k_ref, v_ref, bias_ref, o_ref, s_ref, *, rows, group):
    nkeys = NA_KH * GRID_W
    n_groups = rows // group
    lane = lax.broadcasted_iota(jnp.int32, (1, PAIR_W), 1)
    head0 = lane < DH_A

    def tokens(row, n):
        return pl.ds(pl.multiple_of(row * GRID_W, GRID_W), n)

    def row_start(r):
        return jnp.clip(r - NA_KH // 2, 0, rows - NA_KH)

    def scores(g, slot):
        for t in range(group):
            r = g * group + t
            rs = row_start(r)
            q = q_ref[tokens(r, GRID_W), :]
            zero = jnp.zeros_like(q)
            qm = jnp.concatenate([jnp.where(head0, q, zero), jnp.where(head0, zero, q)], axis=0)
            s = lax.dot_general(qm, k_ref[tokens(rs, nkeys), :], (((1,), (1,)), ((), ())),
                                preferred_element_type=F32)
            s_ref[slot, t] = s + bias_ref[r - rs]

    def attend(g, slot):
        for t in range(group):
            r = g * group + t
            s = s_ref[slot, t]
            m = jnp.max(s, axis=-1, keepdims=True)
            p = jnp.exp2(s - m)
            vw = v_ref[tokens(row_start(r), nkeys), :]
            o2 = _dot(p.astype(BF16), jnp.concatenate([vw, jnp.ones_like(vw)], axis=1))
            o = o2[:, :PAIR_W] / o2[:, PAIR_W:]
            o_ref[tokens(r, GRID_W), :] = jnp.where(head0, o[:GRID_W], o[GRID_W:])

    scores(0, 0)

    def body(j, carry):
        attend(2 * j, 0)
        scores(2 * j + 1, 1)
        attend(2 * j + 1, 1)
        scores(jnp.minimum(2 * j + 2, n_groups - 1), 0)
        return carry

    lax.fori_loop(0, n_groups // 2, body, 0)


def _natten(qkvg, bias, batch, seq):
    rows = seq // GRID_W
    group = _pick(rows // 2, 8)
    assert seq % GRID_W == 0 and rows >= NA_KH, "token grid must hold a full attention window"
    assert rows % (2 * group) == 0

    def qkv_spec(slot):
        return pl.BlockSpec((None, None, seq, PAIR_W), lambda b, p: (slot, p, b, 0))

    return pl.pallas_call(
        functools.partial(_natten_kernel, rows=rows, group=group),
        grid=(batch, N_PAIR),
        in_specs=[
            qkv_spec(0), qkv_spec(1), qkv_spec(2),
            pl.BlockSpec((None, NA_KH, PAIR_W, NA_KH * GRID_W), lambda b, p: (p, 0, 0, 0)),
        ],
        out_specs=pl.BlockSpec((None, seq, PAIR_W), lambda b, p: (p, b, 0)),
        out_shape=jax.ShapeDtypeStruct((N_PAIR, batch * seq, PAIR_W), F32),
        scratch_shapes=[pltpu.VMEM((2, group, PAIR_W, NA_KH * GRID_W), F32)],
        compiler_params=pltpu.CompilerParams(
            dimension_semantics=("arbitrary", "arbitrary"),
            vmem_limit_bytes=VMEM_LIMIT),
        name="natten",
    )(qkvg, qkvg, qkvg, bias)


def _attn_bias_kernel(rpb_ref, o_ref):
    lane1 = lax.broadcasted_iota(jnp.int32, (1, LANES), 1)
    lane = lax.broadcasted_iota(jnp.int32, (GRID_W, LANES), 1)
    qcol = lax.broadcasted_iota(jnp.int32, (GRID_W, LANES), 0)
    kcol = lane & (GRID_W - 1)
    col_start = jnp.clip(qcol - NA_KW // 2, 0, GRID_W - NA_KW)
    masked = jnp.where((kcol >= col_start) & (kcol < col_start + NA_KW), 0.0, -jnp.inf)
    left = lane < GRID_W
    n_off = NA_KW - 1
    near = (lane1 <= n_off) | (lane1 >= LANES - n_off)
    for hh in range(2):
        r = rpb_ref[hh]
        far = jnp.where(lane1 < GRID_W, r[:, 2 * n_off:2 * n_off + 1], r[:, 0:1])
        g = jnp.where(near, pltpu.roll(r, LANES - n_off, 1), far)
        lo, hi = [], []
        for ri in range(2 * NA_KH - 1):
            gb = jnp.broadcast_to(g[ri:ri + 1, :], (GRID_W, LANES))
            t = pltpu.roll(gb, 0, 1, stride=1, stride_axis=0)
            lo.append(t)
            hi.append(pltpu.roll(t, GRID_W, 1))
        for var in range(NA_KH):
            for j in range(NA_KH // 2):
                ra = 2 * j - var + NA_KH - 1
                o_ref[var, hh * GRID_W:(hh + 1) * GRID_W, j * LANES:(j + 1) * LANES] = (
                    jnp.where(left, lo[ra], hi[ra + 1]) * LOG2E + masked)


def _attn_bias(rpb):
    h, nr, nc = rpb.shape
    rpb = jnp.pad(rpb.astype(F32), ((0, 0), (0, 2 * NA_KH - nr), (0, LANES - nc)))
    return pl.pallas_call(
        _attn_bias_kernel,
        grid=(N_PAIR,),
        in_specs=[pl.BlockSpec((2, 2 * NA_KH, LANES), lambda p: (p, 0, 0))],
        out_specs=pl.BlockSpec((None, NA_KH, PAIR_W, NA_KH * GRID_W), lambda p: (p, 0, 0, 0)),
        out_shape=jax.ShapeDtypeStruct((N_PAIR, NA_KH, PAIR_W, NA_KH * GRID_W), F32),
        compiler_params=pltpu.CompilerParams(dimension_semantics=("arbitrary",)),
        name="attn_bias",
    )(rpb)


def _mix_out_kernel(a_ref, u_ref, g_ref, x_ref, ws_ref, bs_ref, na_ref, nb_ref, wo_ref,
                    o_ref, sg_ref):
    n_pair, tm, _ = a_ref.shape
    da = n_pair * PAIR_W
    lane = lax.broadcasted_iota(jnp.int32, (1, PAIR_W), 1)
    head0 = lane < DH_B
    n_chunk = tm // CHUNK
    for c0 in range(0, n_chunk, 2):
        chunks = range(c0, min(c0 + 2, n_chunk))
        for p in range(H_B // 2):
            csl = slice(p * PAIR_W, (p + 1) * PAIR_W)
            g = jnp.concatenate([g_ref[p, c * CHUNK:(c + 1) * CHUNK, :] for c in chunks], axis=1)
            zero = jnp.zeros_like(g)
            first = jnp.concatenate([head0] * len(chunks), axis=1)
            gg = jnp.concatenate([jnp.where(first, g, zero), jnp.where(first, zero, g)], axis=0)
            mixed = _dot(ws_ref[p], gg)
            for t, c in enumerate(chunks):
                rsl = slice(c * CHUNK, (c + 1) * CHUNK)
                sg_ref[rsl, csl] = u_ref[rsl, csl] * (mixed[:, t * PAIR_W:(t + 1) * PAIR_W] + bs_ref[p])
    a = jnp.concatenate([a_ref[p] for p in range(n_pair)], axis=1)
    an = _rms(a, na_ref[...]).astype(BF16)
    sn = _rms(sg_ref[...], nb_ref[...]).astype(BF16)
    o_ref[...] = x_ref[...] + _dot(an, wo_ref[:da, :]) + _dot(sn, wo_ref[da:, :])


def _mix_out(a, u, qkvg, x, ws, bs, na, nb, wo):
    n, d = x.shape
    n_pair = a.shape[0]
    da = n_pair * PAIR_W
    db = u.shape[1]
    tm = _pick(n, 512)
    return pl.pallas_call(
        _mix_out_kernel,
        grid=(n // tm,),
        in_specs=[
            pl.BlockSpec((n_pair, tm, PAIR_W), lambda i: (0, i, 0)),
            pl.BlockSpec((tm, db), lambda i: (i, 0)),
            pl.BlockSpec((None, db // PAIR_W, tm, PAIR_W), lambda i: (3, 0, i, 0)),
            pl.BlockSpec((tm, d), lambda i: (i, 0)),
            pl.BlockSpec(ws.shape, lambda i: (0, 0, 0)),
            pl.BlockSpec(bs.shape, lambda i: (0, 0, 0)),
            pl.BlockSpec((1, da), lambda i: (0, 0)),
            pl.BlockSpec((1, db), lambda i: (0, 0)),
            pl.BlockSpec(wo.shape, lambda i: (0, 0)),
        ],
        out_specs=pl.BlockSpec((tm, d), lambda i: (i, 0)),
        out_shape=jax.ShapeDtypeStruct((n, d), F32),
        scratch_shapes=[pltpu.VMEM((tm, db), F32)],
        compiler_params=pltpu.CompilerParams(
            dimension_semantics=("arbitrary",),
            vmem_limit_bytes=VMEM_LIMIT),
        name="mix_out",
    )(a, u, qkvg, x, ws, bs, na, nb, wo)


def _layer(x, p):
    b, t, d = x.shape
    x = x.reshape(b * t, d)
    x1 = _ffn(x, p["ffn1_norm"], p["ffn1_w_in"], p["ffn1_w_out"])
    qkvg, u = _mix_in(x1, p["mix_norm"], p["w_in_mix"], p["q_norm"], p["k_norm"],
                      p["gate_norm"])
    a = _natten(qkvg, p["attn_bias"], b, t)
    x2 = _mix_out(a, u, qkvg, x1, p["w_spatial"], p["b_spatial"], p["out_norm_a"],
                  p["out_norm_b"], p["w_out_mix"])
    y = _ffn(x2, p["ffn2_norm"], p["ffn2_w_in"], p["ffn2_w_out"], p["final_norm"])
    return y.reshape(b, t, d)


def _prep(l, ffn1_norm, ffn1_w_in, ffn1_w_out, mix_norm, w_in_mix, q_norm, k_norm, attn_rpb,
          gate_norm, w_spatial, b_spatial, out_norm_a, out_norm_b, w_out_mix, ffn2_norm,
          ffn2_w_in, ffn2_w_out, final_norm):
    row = lambda v: v[l].astype(F32).reshape(1, -1)
    lane = jnp.arange(PAIR_W)
    bs = b_spatial[l].astype(F32).reshape(H_B // 2, 2, CHUNK)[:, lane // DH_B, :].transpose(0, 2, 1)
    return dict(
        ffn1_norm=row(ffn1_norm), ffn1_w_in=ffn1_w_in[l].astype(BF16),
        ffn1_w_out=ffn1_w_out[l],
        mix_norm=row(mix_norm), w_in_mix=w_in_mix[l].astype(BF16),
        q_norm=jnp.tile(row(q_norm), (1, H_A)), k_norm=jnp.tile(row(k_norm), (1, H_A)),
        attn_bias=_attn_bias(attn_rpb[l]), gate_norm=row(gate_norm),
        w_spatial=w_spatial[l].astype(BF16).reshape(H_B // 2, 2, CHUNK, CHUNK).transpose(
            0, 2, 1, 3).reshape(H_B // 2, CHUNK, 2 * CHUNK),
        b_spatial=bs, out_norm_a=row(out_norm_a), out_norm_b=row(out_norm_b),
        w_out_mix=w_out_mix[l].astype(BF16),
        ffn2_norm=row(ffn2_norm), ffn2_w_in=ffn2_w_in[l].astype(BF16),
        ffn2_w_out=ffn2_w_out[l], final_norm=row(final_norm),
    )


def kernel(x_prompt, x_sample, ffn1_norm, ffn1_w_in, ffn1_w_out, mix_norm, w_in_mix, q_norm, k_norm, attn_rpb, gate_norm, w_spatial, b_spatial, out_norm_a, out_norm_b, w_out_mix, ffn2_norm, ffn2_w_in, ffn2_w_out, final_norm):
    weights = (ffn1_norm, ffn1_w_in, ffn1_w_out, mix_norm, w_in_mix, q_norm, k_norm, attn_rpb,
               gate_norm, w_spatial, b_spatial, out_norm_a, out_norm_b, w_out_mix, ffn2_norm,
               ffn2_w_in, ffn2_w_out, final_norm)
    depth = ffn1_norm.shape[0]
    layers = [_prep(l, *weights) for l in range(depth)]

    def run(x):
        for p in layers:
            x = _layer(x, p)
        return x

    return (run(x_prompt), run(x_sample))
```
